```python
import jax
import jax.numpy as jnp
from jax import lax
import numpy as np

D_MODEL = 1024
BATCH = 32
SEQ = 2048
DEPTH = 1

CHUNK = 64
A_HEADS = 8
A_HEAD_DIM = 64
KV_RANK = 128
IDX_HEADS = 8
IDX_DIM = 64
MAX_SELECTED = 256
M_HEADS = 4
M_HEAD_DIM = 128
CONV_WIDTH = 4
N_EXPERTS = 32
TOP_EXPERTS = 4
D_EXPERT = D_MODEL
SWIGLU_LIMIT = 7.0
SWIGLU_ALPHA = 1.702
EXPERT_BLOCK = 128
LN_EPS = 1e-5
DN_ALPHA = (2.0 * DEPTH) ** 0.25
DN_BETA = (8.0 * DEPTH) ** -0.25

A_WIDTH = A_HEADS * A_HEAD_DIM
M_WIDTH = M_HEADS * M_HEAD_DIM
MIX_WIDTH = A_WIDTH + M_WIDTH
IN_SIZES = (A_WIDTH, KV_RANK, IDX_HEADS * IDX_DIM, IDX_DIM, IDX_HEADS,
            M_WIDTH, M_WIDTH, M_WIDTH, M_HEADS, M_HEADS)
IN_WIDTH = sum(IN_SIZES)
IN_SPLITS = tuple(int(v) for v in np.cumsum(IN_SIZES)[:-1])

kernel_name = "hybrid_dsa_mlstm_moe_deepnorm"


def layer_norm(x, g, b):
    xf = x.astype(jnp.float32)
    mu = xf.mean(-1, keepdims=True)
    var = jnp.square(xf - mu).mean(-1, keepdims=True)
    return ((xf - mu) * lax.rsqrt(var + LN_EPS) * g + b).astype(x.dtype)


def rms_norm(x, g):
    xf = x.astype(jnp.float32)
    ms = jnp.square(xf).mean(-1, keepdims=True)
    return (xf * lax.rsqrt(ms + LN_EPS) * g).astype(x.dtype)


def alibi_slopes(n):
    return jnp.asarray(2.0 ** (-8.0 * np.arange(1, n + 1) / n), dtype=jnp.float32)


def to_chunks(t, nc):
    return jnp.moveaxis(t.reshape(t.shape[0], nc, CHUNK, *t.shape[2:]), 1, 0)


def dsa_attention(q, c_kv, q_idx, k_idx, w_idx, kv_norm_g, w_uk, w_uv):
    B, S = q.shape[:2]
    nc = S // CHUNK
    k_sel = min(MAX_SELECTED, S // 4)
    c = rms_norm(c_kv, kv_norm_g)
    q_lat = jnp.einsum('bshd,hrd->bshr', q, w_uk) * (A_HEAD_DIM ** -0.5)
    slopes = alibi_slopes(A_HEADS)
    key_chunk = jnp.arange(S) // CHUNK

    def chunk_block(args):
        ci, q_b, qi_b, wi_b = args
        qpos = ci * CHUNK + jnp.arange(CHUNK)
        rel = jax.nn.relu(jnp.einsum('bthd,bsd->bths', qi_b, k_idx))
        score = jnp.einsum('bths,bth->bts', rel, wi_b)
        score = jnp.where((key_chunk <= ci)[None, None, :], score, -jnp.inf)
        top_s, top_i = lax.top_k(score, k_sel)
        valid = jnp.isfinite(top_s)
        c_sel = jax.vmap(lambda cb, ib: cb[ib])(c, top_i)
        logits = jnp.einsum('bthr,btkr->bthk', q_b, c_sel).astype(jnp.float32)
        dist = jnp.abs(qpos[None, :, None] - top_i).astype(jnp.float32)
        logits = logits - slopes[None, None, :, None] * dist[:, :, None, :]
        logits = jnp.where(valid[:, :, None, :], logits, -jnp.inf)
        p = jax.nn.softmax(logits, axis=-1).astype(c_sel.dtype)
        return jnp.einsum('bthk,btkr->bthr', p, c_sel)

    xs = (jnp.arange(nc), to_chunks(q_lat, nc), to_chunks(q_idx, nc), to_chunks(w_idx, nc))
    o_lat = lax.map(chunk_block, xs)
    o_lat = jnp.moveaxis(o_lat, 0, 1).reshape(B, S, A_HEADS, KV_RANK)
    o = jnp.einsum('bshr,hrd->bshd', o_lat, w_uv)
    return o.reshape(B, S, A_WIDTH)


def causal_conv(u, w, b):
    y = lax.conv_general_dilated(u, w[:, None, :], window_strides=(1,),
                                 padding=[(CONV_WIDTH - 1, 0)],
                                 dimension_numbers=('NWC', 'WIO', 'NWC'),
                                 feature_group_count=u.shape[-1])
    return y + b


def mlstm(u, v, z, i_pre, f_pre, conv_w, conv_b, w_mq, w_mk, norm_g):
    B, S, _ = u.shape
    nc = S // CHUNK
    f32 = jnp.float32
    uc = jax.nn.silu(causal_conv(u, conv_w, conv_b)).reshape(B, S, M_HEADS, M_HEAD_DIM)
    q = jnp.einsum('bshd,hde->bhse', uc, w_mq).astype(f32)
    k = (jnp.einsum('bshd,hde->bhse', uc, w_mk) * (M_HEAD_DIM ** -0.5)).astype(f32)
    vh = v.reshape(B, S, M_HEADS, M_HEAD_DIM).transpose(0, 2, 1, 3).astype(f32)
    chunked = lambda t: t.reshape(B, M_HEADS, nc, CHUNK, *t.shape[3:])
    qc, kc, vc = chunked(q), chunked(k), chunked(vh)
    li = chunked(i_pre.astype(f32).transpose(0, 2, 1))
    lf = chunked(jax.nn.log_sigmoid(f_pre.astype(f32)).transpose(0, 2, 1))
    b = jnp.cumsum(lf, axis=-1)
    b_tot = b[..., -1]
    a = b_tot[..., None] - b + li
    a_max = a.max(-1)

    def step(carry, inp):
        C, n, m = carry
        kk, vv, aa, amax, btot = inp
        m_new = jnp.maximum(btot + m, amax)
        decay = jnp.exp(btot + m - m_new)
        w = jnp.exp(aa - m_new[..., None])
        C_new = decay[..., None, None] * C + jnp.einsum('bhl,bhlv,bhlk->bhvk', w, vv, kk)
        n_new = decay[..., None] * n + jnp.einsum('bhl,bhlk->bhk', w, kk)
        return (C_new, n_new, m_new), (C, n, m)

    init = (jnp.zeros((B, M_HEADS, M_HEAD_DIM, M_HEAD_DIM), f32),
            jnp.zeros((B, M_HEADS, M_HEAD_DIM), f32),
            jnp.full((B, M_HEADS), -1e30, f32))
    mv = lambda t: jnp.moveaxis(t, 2, 0)
    _, (C_prev, n_prev, m_prev) = lax.scan(step, init, (mv(kc), mv(vc), mv(a), mv(a_max), mv(b_tot)))
    C_prev = jnp.moveaxis(C_prev, 0, 2)
    n_prev = jnp.moveaxis(n_prev, 0, 2)
    m_prev = jnp.moveaxis(m_prev, 0, 2)

    g = b + m_prev[..., None]
    causal = jnp.tril(jnp.ones((CHUNK, CHUNK), bool))
    dlog = jnp.where(causal, b[..., :, None] - b[..., None, :] + li[..., None, :], -jnp.inf)
    m_t = jnp.maximum(g, dlog.max(-1))
    w_intra = jnp.exp(dlog - m_t[..., None])
    w_state = jnp.exp(g - m_t)
    s = jnp.einsum('bhcte,bhcse->bhcts', qc, kc) * w_intra
    num = (jnp.einsum('bhcts,bhcsv->bhctv', s, vc)
           + w_state[..., None] * jnp.einsum('bhcvk,bhctk->bhctv', C_prev, qc))
    den = s.sum(-1) + w_state * jnp.einsum('bhck,bhctk->bhct', n_prev, qc)
    hc = num / jnp.maximum(jnp.abs(den), jnp.exp(-m_t))[..., None]
    hc = hc.reshape(B, M_HEADS, S, M_HEAD_DIM)
    mu = hc.mean(-1, keepdims=True)
    var = jnp.square(hc - mu).mean(-1, keepdims=True)
    hc = (hc - mu) * lax.rsqrt(var + LN_EPS)
    h = hc.transpose(0, 2, 1, 3).reshape(B, S, M_WIDTH) * norm_g
    return (jax.nn.sigmoid(z.astype(f32)) * h).astype(u.dtype)


def moe(x, w_router, b_router, w_gu, b_gu, w_down, b_down):
    B, S, D = x.shape
    T = B * S
    xf = x.reshape(T, D)
    logits = (xf @ w_router + b_router).astype(jnp.float32)
    top_v, top_e = lax.top_k(logits, TOP_EXPERTS)
    gates = jax.nn.softmax(top_v, axis=-1)
    A = T * TOP_EXPERTS
    flat_e = top_e.reshape(A)
    flat_tok = jnp.arange(A, dtype=jnp.int32) // TOP_EXPERTS
    flat_g = gates.reshape(A)
    order = jnp.argsort(flat_e)
    se = flat_e[order]
    counts = jnp.bincount(flat_e, length=N_EXPERTS)
    padded = (counts + EXPERT_BLOCK - 1) // EXPERT_BLOCK * EXPERT_BLOCK
    start = jnp.cumsum(counts) - counts
    pad_end = jnp.cumsum(padded)
    pad_start = pad_end - padded
    dest = pad_start[se] + jnp.arange(A, dtype=jnp.int32) - start[se]
    P = A + N_EXPERTS * EXPERT_BLOCK
    nb = P // EXPERT_BLOCK
    row_tok = jnp.zeros((P,), jnp.int32).at[dest].set(flat_tok[order])
    row_gate = jnp.zeros((P,), jnp.float32).at[dest].set(flat_g[order])
    blk_e = jnp.minimum(jnp.searchsorted(pad_end, jnp.arange(nb) * EXPERT_BLOCK, side='right'),
                        N_EXPERTS - 1)

    def expert_block(args):
        e, tok, gt = args
        xb = xf[tok]
        gu = xb @ w_gu[e] + b_gu[e]
        gate, up = gu[:, :D_EXPERT], gu[:, D_EXPERT:]
        gate = jnp.minimum(gate, SWIGLU_LIMIT)
        up = jnp.clip(up, -SWIGLU_LIMIT, SWIGLU_LIMIT)
        h = (up + 1.0) * gate * jax.nn.sigmoid(SWIGLU_ALPHA * gate)
        y = h @ w_down[e] + b_down[e]
        return y * gt[:, None].astype(y.dtype)

    ys = lax.map(expert_block, (blk_e, row_tok.reshape(nb, EXPERT_BLOCK),
                                row_gate.reshape(nb, EXPERT_BLOCK)))
    y = jax.ops.segment_sum(ys.reshape(P, D), row_tok, num_segments=T)
    return y.reshape(B, S, D).astype(x.dtype)


def setup_inputs(seed: int = 0) -> dict:
    key = jax.random.key(seed)
    ks = jax.random.split(key, 24)
    nrm = lambda k, shape, scale: jax.random.normal(k, shape, jnp.float32) * scale
    L = DEPTH
    return {
        "x": nrm(ks[0], (BATCH, SEQ, D_MODEL), 1.0),
        "w_in": nrm(ks[1], (L, D_MODEL, IN_WIDTH), D_MODEL ** -0.5),
        "kv_norm_g": 1.0 + nrm(ks[2], (L, KV_RANK), 0.05),
        "w_uk": nrm(ks[3], (L, A_HEADS, KV_RANK, A_HEAD_DIM), KV_RANK ** -0.5),
        "w_uv": nrm(ks[4], (L, A_HEADS, KV_RANK, A_HEAD_DIM), KV_RANK ** -0.5),
        "m_conv_w": nrm(ks[5], (L, CONV_WIDTH, M_WIDTH), CONV_WIDTH ** -0.5),
        "m_conv_b": nrm(ks[6], (L, M_WIDTH), 0.01),
        "w_mq": nrm(ks[7], (L, M_HEADS, M_HEAD_DIM, M_HEAD_DIM), M_HEAD_DIM ** -0.5),
        "w_mk": nrm(ks[8], (L, M_HEADS, M_HEAD_DIM, M_HEAD_DIM), M_HEAD_DIM ** -0.5),
        "b_i": nrm(ks[9], (L, M_HEADS), 0.1),
        "b_f": jnp.linspace(3.0, 6.0, M_HEADS)[None, :] + nrm(ks[10], (L, M_HEADS), 0.1),
        "m_norm_g": 1.0 + nrm(ks[11], (L, M_WIDTH), 0.05),
        "w_out": nrm(ks[12], (L, MIX_WIDTH, D_MODEL), DN_BETA * MIX_WIDTH ** -0.5),
        "ln1_g": 1.0 + nrm(ks[13], (L, D_MODEL), 0.05),
        "ln1_b": nrm(ks[14], (L, D_MODEL), 0.01),
        "w_router": nrm(ks[15], (L, D_MODEL, N_EXPERTS), D_MODEL ** -0.5),
        "b_router": nrm(ks[16], (L, N_EXPERTS), 0.01),
        "w_gu": nrm(ks[17], (L, N_EXPERTS, D_MODEL, 2 * D_EXPERT), D_MODEL ** -0.5),
        "b_gu": nrm(ks[18], (L, N_EXPERTS, 2 * D_EXPERT), 0.01),
        "w_down": nrm(ks[19], (L, N_EXPERTS, D_EXPERT, D_MODEL), DN_BETA * D_EXPERT ** -0.5),
        "b_down": nrm(ks[20], (L, N_EXPERTS, D_MODEL), 0.01),
        "ln2_g": 1.0 + nrm(ks[21], (L, D_MODEL), 0.05),
        "ln2_b": nrm(ks[22], (L, D_MODEL), 0.01),
    }


def reference(x, w_in, kv_norm_g, w_uk, w_uv, m_conv_w, m_conv_b, w_mq, w_mk, b_i, b_f,
              m_norm_g, w_out, ln1_g, ln1_b, w_router, b_router, w_gu, b_gu, w_down, b_down,
              ln2_g, ln2_b):
    B, S, _ = x.shape
    h = x
    for l in range(DEPTH):
        proj = h @ w_in[l]
        q_a, ckv, q_i, k_i, w_i, u, v, z, i_pre, f_pre = jnp.split(proj, IN_SPLITS, axis=-1)
        a_out = dsa_attention(q_a.reshape(B, S, A_HEADS, A_HEAD_DIM), ckv,
                              q_i.reshape(B, S, IDX_HEADS, IDX_DIM), k_i, w_i,
                              kv_norm_g[l], w_uk[l], w_uv[l])
        m_out = mlstm(u, v, z, i_pre + b_i[l], f_pre + b_f[l], m_conv_w[l], m_conv_b[l],
                      w_mq[l], w_mk[l], m_norm_g[l])
        mix = jnp.concatenate([a_out, m_out], axis=-1) @ w_out[l]
        h = layer_norm(DN_ALPHA * h + mix, ln1_g[l], ln1_b[l])
        ffn = moe(h, w_router[l], b_router[l], w_gu[l], b_gu[l], w_down[l], b_down[l])
        h = layer_norm(DN_ALPHA * h + ffn, ln2_g[l], ln2_b[l])
    return h
```

```python
import functools

import jax
import jax.numpy as jnp
import numpy as np
from jax import lax
from jax.experimental import pallas as pl
from jax.experimental.pallas import tpu as pltpu

F32 = jnp.float32
BF16 = jnp.bfloat16
I32 = jnp.int32

D_MODEL = 1024
SEQ = 2048
CHUNK = 64
A_HEADS = 8
A_HEAD_DIM = 64
KV_RANK = 128
IDX_HEADS = 8
IDX_DIM = 64
K_SEL = 256
M_HEADS = 4
M_HEAD_DIM = 128
CONV_WIDTH = 4
N_EXPERTS = 32
TOP_EXPERTS = 4
SWIGLU_LIMIT = 7.0
SWIGLU_ALPHA = 1.702
LN_EPS = 1e-5
DN_ALPHA = 2.0 ** 0.25

A_WIDTH = A_HEADS * A_HEAD_DIM
M_WIDTH = M_HEADS * M_HEAD_DIM
N_CHUNKS = SEQ // CHUNK

LANES = 128
NEG_BIG = -1e30
INT_MIN = -(2 ** 31)

PROJ_ROWS = 512
KEY_BLOCK = 256
EXPERT_ROWS = 256
COMBINE_ROWS = 256
VMEM_LIMIT = 56 * 1024 * 1024

_QA, _CKV, _QI, _U, _V, _Z = 0, 512, 640, 1152, 1664, 2176
MAIN_WIDTH = 2688


def _dot(a, b):
    return jnp.dot(a, b, preferred_element_type=F32)


def _dot_nt(a, b):
    return lax.dot_general(a, b, (((1,), (1,)), ((), ())), preferred_element_type=F32)


def _proj_kernel(x_ref, wmain_ref, wsmall_ref, wsmallt_ref, wukt_ref, kvg_ref, convw_ref, convb_ref,
                 wmq_ref, wmk_ref,
                 qlat_ref, c_ref, qidx_ref, small_ref, smallt_ref, qm_ref, km_ref, v_ref, z_ref,
                 uext_ref):
    i = pl.program_id(0)
    rows = x_ref.shape[0]
    xb = x_ref[...].astype(BF16)

    qa = _dot(xb, wmain_ref[:, _QA:_CKV])
    for h in range(A_HEADS):
        qh = qa[:, h * A_HEAD_DIM:(h + 1) * A_HEAD_DIM].astype(BF16)
        ql = _dot(qh, wukt_ref[h]) * (A_HEAD_DIM ** -0.5)
        qlat_ref[:, h * KV_RANK:(h + 1) * KV_RANK] = ql.astype(BF16)

    ckv = _dot(xb, wmain_ref[:, _CKV:_QI])
    ms = jnp.mean(ckv * ckv, axis=-1, keepdims=True)
    c_ref[...] = (ckv * lax.rsqrt(ms + LN_EPS) * kvg_ref[...]).astype(BF16)

    qidx_ref[...] = _dot(xb, wmain_ref[:, _QI:_U]).astype(BF16)
    v_ref[...] = _dot(xb, wmain_ref[:, _V:_Z]).astype(BF16)
    z_ref[...] = _dot(xb, wmain_ref[:, _Z:MAIN_WIDTH])
    small_ref[...] = _dot(xb, wsmall_ref[...])
    smallt_ref[...] = _dot_nt(wsmallt_ref[...], xb)

    u = _dot(xb, wmain_ref[:, _U:_V])

    @pl.when(i % (SEQ // rows) == 0)
    def _():
        uext_ref[0:8, :] = jnp.zeros((8, M_WIDTH), F32)

    uext_ref[8:8 + rows, :] = u
    acc = jnp.broadcast_to(convb_ref[...], (rows, M_WIDTH))
    for j in range(CONV_WIDTH):
        off = 8 - (CONV_WIDTH - 1) + j
        acc = acc + convw_ref[j:j + 1, :] * uext_ref[off:off + rows, :]
    uext_ref[0:8, :] = u[rows - 8:rows, :]
    uc = acc * jax.nn.sigmoid(acc)
    for h in range(M_HEADS):
        sl = slice(h * M_HEAD_DIM, (h + 1) * M_HEAD_DIM)
        uh = uc[:, sl].astype(BF16)
        qm_ref[:, sl] = _dot(uh, wmq_ref[h]).astype(BF16)
        km_ref[:, sl] = (_dot(uh, wmk_ref[h]) * (M_HEAD_DIM ** -0.5)).astype(BF16)


def _proj_call(x2, wmain, wsmall, wsmallt, wukt, kvg, convw, convb, wmq, wmk):
    t = x2.shape[0]
    rows = PROJ_ROWS
    full = lambda a: pl.BlockSpec(a.shape, lambda i: (0,) * a.ndim)
    tok = lambda w: pl.BlockSpec((rows, w), lambda i: (i, 0))
    out_shape = (
        jax.ShapeDtypeStruct((t, A_HEADS * KV_RANK), BF16),
        jax.ShapeDtypeStruct((t, KV_RANK), BF16),
        jax.ShapeDtypeStruct((t, IDX_HEADS * IDX_DIM), BF16),
        jax.ShapeDtypeStruct((t, LANES), F32),
        jax.ShapeDtypeStruct((LANES, t), F32),
        jax.ShapeDtypeStruct((t, M_WIDTH), BF16),
        jax.ShapeDtypeStruct((t, M_WIDTH), BF16),
        jax.ShapeDtypeStruct((t, M_WIDTH), BF16),
        jax.ShapeDtypeStruct((t, M_WIDTH), F32),
    )
    out_specs = (tok(A_HEADS * KV_RANK), tok(KV_RANK), tok(IDX_HEADS * IDX_DIM), tok(LANES),
                 pl.BlockSpec((LANES, rows), lambda i: (0, i)),
                 tok(M_WIDTH), tok(M_WIDTH), tok(M_WIDTH), tok(M_WIDTH))
    return pl.pallas_call(
        _proj_kernel,
        grid=(t // rows,),
        in_specs=[tok(D_MODEL), full(wmain), full(wsmall), full(wsmallt), full(wukt), full(kvg),
                  full(convw), full(convb), full(wmq), full(wmk)],
        out_specs=out_specs,
        out_shape=out_shape,
        scratch_shapes=[pltpu.VMEM((8 + rows, M_WIDTH), F32)],
        compiler_params=pltpu.CompilerParams(dimension_semantics=("arbitrary",),
                                             vmem_limit_bytes=VMEM_LIMIT),
        name="proj",
    )(x2, wmain, wsmall, wsmallt, wukt, kvg, convw, convb, wmq, wmk)


def _count_rows(keys_ref, nkb, pred):
    def body(kb, acc):
        key = keys_ref[kb]
        pos = kb * KEY_BLOCK + lax.broadcasted_iota(I32, (CHUNK, KEY_BLOCK), 1)
        return acc + jnp.where(pred(key, pos), 1.0, 0.0)
    acc = lax.fori_loop(0, nkb, body, jnp.zeros((CHUNK, KEY_BLOCK), F32))
    return jnp.sum(acc, axis=-1, keepdims=True)


def _dsa_kernel(qi_ref, wcol_ref, slope_ref, qlat_ref, kidxt_ref, c_ref, wuv_ref,
                aout_ref,
                kt_ref, keys_ref, thr_ref, jlim_ref):
    ci = pl.program_id(1)
    hrows = A_HEADS * CHUNK

    @pl.when(ci == 0)
    def _():
        for kb in range(SEQ // KEY_BLOCK):
            kt_ref[kb] = kidxt_ref[:, kb * KEY_BLOCK:(kb + 1) * KEY_BLOCK].astype(BF16)

    nkb = ci // (KEY_BLOCK // CHUNK) + 1
    nkeys = (ci + 1) * CHUNK
    qi = qi_ref[0, 0]
    wcol = wcol_ref[0, 0]

    def score_body(kb, carry):
        rel = jnp.maximum(_dot(qi, kt_ref[kb]), 0.0) * wcol
        sc = rel[0:CHUNK]
        for h in range(1, IDX_HEADS):
            sc = sc + rel[h * CHUNK:(h + 1) * CHUNK]
        sc = jnp.where(sc == 0.0, 0.0, sc)
        bits = pltpu.bitcast(sc, I32)
        key = jnp.where(bits < 0, bits ^ jnp.int32(0x7FFFFFFF), bits)
        pos = kb * KEY_BLOCK + lax.broadcasted_iota(I32, (CHUNK, KEY_BLOCK), 1)
        keys_ref[kb] = jnp.where(pos < nkeys, key, jnp.int32(INT_MIN))
        return carry

    lax.fori_loop(0, nkb, score_body, 0)

    thr_ref[...] = jnp.full(thr_ref.shape, INT_MIN + 1, I32)
    jlim_ref[...] = jnp.full(jlim_ref.shape, SEQ - 1, I32)

    @pl.when(nkeys > K_SEL)
    def _():
        k_sel = jnp.float32(K_SEL)
        cnt0 = _count_rows(keys_ref, nkb, lambda key, pos: key >= 0)
        t0 = jnp.where(cnt0 >= k_sel, jnp.int32(0), jnp.int32(INT_MIN))

        def bit_body(p, t):
            cand = t + lax.shift_left(jnp.int32(1), jnp.int32(30) - p)
            cnt = _count_rows(keys_ref, nkb, lambda key, pos: key >= cand)
            return jnp.where(cnt >= k_sel, cand, t)

        t = lax.fori_loop(0, 31, bit_body, t0)
        thr_ref[...] = jnp.broadcast_to(t, thr_ref.shape)
        cnt_ge = _count_rows(keys_ref, nkb, lambda key, pos: key >= t)

        @pl.when(jnp.max(cnt_ge) > k_sel)
        def _():
            cnt_gt = _count_rows(keys_ref, nkb, lambda key, pos: key > t)
            need = k_sel - cnt_gt

            def pos_body(p, j):
                cand = j + lax.shift_left(jnp.int32(1), jnp.int32(10) - p)
                cnt = _count_rows(keys_ref, nkb, lambda key, pos: (key == t) & (pos < cand))
                return jnp.where(cnt < need, cand, j)

            j = lax.fori_loop(0, 11, pos_body, jnp.zeros((CHUNK, 1), I32))
            jlim_ref[...] = jnp.broadcast_to(j, jlim_ref.shape)

    thr = thr_ref[:, 0:1]
    jlim = jlim_ref[:, 0:1]
    qlat = jnp.concatenate(
        [qlat_ref[0, :, h * KV_RANK:(h + 1) * KV_RANK] for h in range(A_HEADS)], axis=0)
    slope = slope_ref[...].reshape(A_HEADS, CHUNK, 1)
    qpos = ci * CHUNK + lax.broadcasted_iota(I32, (CHUNK, KEY_BLOCK), 0)

    def att_body(kb, carry):
        m, l, acc = carry
        off = pl.multiple_of(kb * KEY_BLOCK, KEY_BLOCK)
        cblk = c_ref[0, pl.ds(off, KEY_BLOCK), :]
        s = _dot_nt(qlat, cblk)
        key = keys_ref[kb]
        pos = off + lax.broadcasted_iota(I32, (CHUNK, KEY_BLOCK), 1)
        sel = (key > thr) | ((key == thr) & (pos <= jlim))
        bias = jnp.where(sel, 0.0, NEG_BIG)
        dist = jnp.abs(qpos - pos).astype(F32)
        s = (s.reshape(A_HEADS, CHUNK, KEY_BLOCK) - slope * dist[None] + bias[None]).reshape(hrows, KEY_BLOCK)
        m_new = jnp.maximum(m, jnp.max(s, axis=-1, keepdims=True))
        alpha = jnp.exp(m - m_new)
        p = jnp.exp(s - m_new)
        l_new = alpha * l + jnp.sum(p, axis=-1, keepdims=True)
        acc_new = alpha * acc + _dot(p.astype(BF16), cblk)
        return m_new, l_new, acc_new

    init = (jnp.full((hrows, 1), NEG_BIG, F32), jnp.zeros((hrows, 1), F32),
            jnp.zeros((hrows, KV_RANK), F32))
    _, l, acc = lax.fori_loop(0, nkb, att_body, init)
    o = (acc / l).astype(BF16)
    outs = [_dot(o[h * CHUNK:(h + 1) * CHUNK], wuv_ref[h]) for h in range(A_HEADS)]
    aout_ref[0] = jnp.concatenate(outs, axis=-1).astype(BF16)


def _dsa_call(qi, wcol, slope, qlat, smallt, c, wuv):
    b = qlat.shape[0]
    return pl.pallas_call(
        _dsa_kernel,
        grid=(b, N_CHUNKS),
        in_specs=[
            pl.BlockSpec((1, 1, A_HEADS * CHUNK, IDX_DIM), lambda bi, ci: (bi, ci, 0, 0)),
            pl.BlockSpec((1, 1, A_HEADS * CHUNK, 1), lambda bi, ci: (bi, ci, 0, 0)),
            pl.BlockSpec((A_HEADS * CHUNK, 1), lambda bi, ci: (0, 0)),
            pl.BlockSpec((1, CHUNK, A_HEADS * KV_RANK), lambda bi, ci: (bi, ci, 0)),
            pl.BlockSpec((IDX_DIM, SEQ), lambda bi, ci: (0, bi)),
            pl.BlockSpec((1, SEQ, KV_RANK), lambda bi, ci: (bi, 0, 0)),
            pl.BlockSpec(wuv.shape, lambda bi, ci: (0, 0, 0)),
        ],
        out_specs=pl.BlockSpec((1, CHUNK, A_WIDTH), lambda bi, ci: (bi, ci, 0)),
        out_shape=jax.ShapeDtypeStruct((b, SEQ, A_WIDTH), BF16),
        scratch_shapes=[pltpu.VMEM((SEQ // KEY_BLOCK, IDX_DIM, KEY_BLOCK), BF16),
                        pltpu.VMEM((SEQ // KEY_BLOCK, CHUNK, KEY_BLOCK), I32),
                        pltpu.VMEM((CHUNK, LANES), I32), pltpu.VMEM((CHUNK, LANES), I32)],
        compiler_params=pltpu.CompilerParams(dimension_semantics=("arbitrary", "arbitrary"),
                                             vmem_limit_bytes=VMEM_LIMIT),
        name="dsa",
    )(qi, wcol, slope, qlat, smallt, c, wuv)


def _log_sigmoid(x):
    return jnp.minimum(x, 0.0) - jnp.log(1.0 + jnp.exp(-jnp.abs(x)))


def _mlstm_kernel(qm_ref, km_ref, v_ref, z_ref, gcol_ref, grow_ref, bcol_ref, brow_ref, normg_ref,
                  out_ref,
                  ct_ref, n_ref, m_ref):
    ci = pl.program_id(1)

    @pl.when(ci == 0)
    def _():
        ct_ref[...] = jnp.zeros(ct_ref.shape, F32)
        n_ref[...] = jnp.zeros(n_ref.shape, F32)
        m_ref[...] = jnp.full(m_ref.shape, NEG_BIG, F32)

    hi = lax.Precision.HIGHEST
    gcol = gcol_ref[0] + bcol_ref[...]
    grow = grow_ref[0] + brow_ref[...]
    r_i = lax.broadcasted_iota(I32, (CHUNK, CHUNK), 0)
    c_i = lax.broadcasted_iota(I32, (CHUNK, CHUNK), 1)
    causal = c_i <= r_i
    tril = jnp.where(causal, 1.0, 0.0).astype(F32)
    triu = jnp.where(r_i <= c_i, 1.0, 0.0).astype(F32)
    lf_col = _log_sigmoid(gcol)
    lf_row = _log_sigmoid(grow)
    bcum_col = jnp.dot(tril, lf_col, precision=hi, preferred_element_type=F32)
    bcum_row = jnp.dot(lf_row, triu, precision=hi, preferred_element_type=F32)

    for h in range(M_HEADS):
        sl = slice(h * M_HEAD_DIM, (h + 1) * M_HEAD_DIM)
        q = qm_ref[0, :, sl]
        k = km_ref[0, :, sl]
        v = v_ref[0, :, sl]
        li_col = gcol[:, 8 + h:9 + h]
        li_row = grow[h:h + 1, :]
        b_col = bcum_col[:, 12 + h:13 + h]
        b_row = bcum_row[4 + h:5 + h, :]
        b_tot = b_row[:, CHUNK - 1:CHUNK]
        ct_prev = ct_ref[h]
        n_prev = n_ref[h]
        m_prev = m_ref[h][:, 0:1]

        g = b_col + m_prev
        dlog = jnp.where(causal, b_col - b_row + li_row, NEG_BIG)
        m_t = jnp.maximum(g, jnp.max(dlog, axis=-1, keepdims=True))
        w_intra = jnp.exp(dlog - m_t)
        w_state = jnp.exp(g - m_t)
        s = _dot_nt(q, k) * w_intra
        num = _dot(s.astype(BF16), v) + w_state * _dot(q, ct_prev.astype(BF16))
        den = (jnp.sum(s, axis=-1, keepdims=True)
               + w_state * jnp.sum(q.astype(F32) * n_prev, axis=-1, keepdims=True))
        hc = num / jnp.maximum(jnp.abs(den), jnp.exp(-m_t))
        mu = jnp.mean(hc, axis=-1, keepdims=True)
        var = jnp.mean(jnp.square(hc - mu), axis=-1, keepdims=True)
        hc = (hc - mu) * lax.rsqrt(var + LN_EPS)
        out_ref[0, :, sl] = (jax.nn.sigmoid(z_ref[0, :, sl]) * (hc * normg_ref[:, sl])).astype(BF16)

        a_col = b_tot - b_col + li_col
        a_max = jnp.max(a_col, axis=0, keepdims=True)
        m_new = jnp.maximum(b_tot + m_prev, a_max)
        decay = jnp.exp(b_tot + m_prev - m_new)
        kw = k.astype(F32) * jnp.exp(a_col - m_new)
        ct_ref[h] = decay * ct_prev + _dot(kw.T.astype(BF16), v)
        n_ref[h] = decay * n_prev + jnp.sum(kw, axis=0, keepdims=True)
        m_ref[h] = jnp.broadcast_to(m_new, m_ref.shape[1:])


def _mlstm_call(qm, km, v, z, gcol, grow, bcol, brow, normg):
    b = qm.shape[0]
    tokspec = lambda w: pl.BlockSpec((1, CHUNK, w), lambda bi, ci: (bi, ci, 0))
    full = lambda a: pl.BlockSpec(a.shape, lambda bi, ci: (0,) * a.ndim)
    return pl.pallas_call(
        _mlstm_kernel,
        grid=(b, N_CHUNKS),
        in_specs=[tokspec(M_WIDTH), tokspec(M_WIDTH), tokspec(M_WIDTH), tokspec(M_WIDTH), tokspec(LANES),
                  pl.BlockSpec((1, 8, CHUNK), lambda bi, ci: (bi * N_CHUNKS + ci, 0, 0)),
                  full(bcol), full(brow), full(normg)],
        out_specs=tokspec(M_WIDTH),
        out_shape=jax.ShapeDtypeStruct((b, SEQ, M_WIDTH), BF16),
        scratch_shapes=[pltpu.VMEM((M_HEADS, M_HEAD_DIM, M_HEAD_DIM), F32),
                        pltpu.VMEM((M_HEADS, 1, M_HEAD_DIM), F32),
                        pltpu.VMEM((M_HEADS, 1, LANES), F32)],
        compiler_params=pltpu.CompilerParams(dimension_semantics=("arbitrary", "arbitrary"),
                                             vmem_limit_bytes=VMEM_LIMIT),
        name="mlstm",
    )(qm, km, v, z, gcol, grow, bcol, brow, normg)


def _layer_norm(y, g, b):
    mu = jnp.mean(y, axis=-1, keepdims=True)
    var = jnp.mean(jnp.square(y - mu), axis=-1, keepdims=True)
    return (y - mu) * lax.rsqrt(var + LN_EPS) * g + b


def _mix_kernel(a_ref, m_ref, x_ref, wout_ref, g_ref, b_ref, wr_ref, br_ref,
                h_ref, e_ref, gate_ref):
    mix = _dot(a_ref[...], wout_ref[0:A_WIDTH, :]) + _dot(m_ref[...], wout_ref[A_WIDTH:, :])
    h1 = _layer_norm(DN_ALPHA * x_ref[...] + mix, g_ref[...], b_ref[...])
    h_ref[...] = h1
    logits = _dot(h1.astype(BF16), wr_ref[...]) + br_ref[...]
    lane = lax.broadcasted_iota(I32, logits.shape, 1).astype(F32)
    e_slab = jnp.zeros(logits.shape, F32)
    v_slab = jnp.zeros(logits.shape, F32)
    vals = []
    for j in range(TOP_EXPERTS):
        mx = jnp.max(logits, axis=-1, keepdims=True)
        idx = jnp.min(jnp.where(logits == mx, lane, float(LANES)), axis=-1, keepdims=True)
        e_slab = jnp.where(lane == float(j), idx, e_slab)
        vals.append(mx)
        logits = jnp.where(lane == idx, -jnp.inf, logits)
    ex = [jnp.exp(vj - vals[0]) for vj in vals]
    tot = ex[0] + ex[1] + ex[2] + ex[3]
    for j in range(TOP_EXPERTS):
        v_slab = jnp.where(lane == float(j), ex[j] / tot, v_slab)
    e_ref[...] = e_slab.astype(I32)
    gate_ref[...] = v_slab


def _mix_call(aout, mout, x2, wout, g, b, wr, br):
    t = x2.shape[0]
    rows = PROJ_ROWS
    full = lambda a: pl.BlockSpec(a.shape, lambda i: (0,) * a.ndim)
    tok = lambda w: pl.BlockSpec((rows, w), lambda i: (i, 0))
    return pl.pallas_call(
        _mix_kernel,
        grid=(t // rows,),
        in_specs=[tok(A_WIDTH), tok(M_WIDTH), tok(D_MODEL), full(wout), full(g), full(b), full(wr), full(br)],
        out_specs=(tok(D_MODEL), tok(LANES), tok(LANES)),
        out_shape=(jax.ShapeDtypeStruct((t, D_MODEL), F32),
                   jax.ShapeDtypeStruct((t, LANES), I32),
                   jax.ShapeDtypeStruct((t, LANES), F32)),
        compiler_params=pltpu.CompilerParams(dimension_semantics=("arbitrary",),
                                             vmem_limit_bytes=VMEM_LIMIT),
        name="mix",
    )(aout, mout, x2, wout, g, b, wr, br)


def _gather_rows(idx_ref, src_hbm, dst_ref, sem, n):
    def body(r, carry):
        pltpu.make_async_copy(src_hbm.at[pl.ds(idx_ref[0, 0, r], 1)], dst_ref.at[pl.ds(r, 1)], sem).start()
        return carry
    lax.fori_loop(0, n, body, 0)


def _gather_wait(src_hbm, dst_ref, sem, n):
    pltpu.make_async_copy(src_hbm.at[pl.ds(0, n)], dst_ref, sem).wait()


def _experts_kernel(blk_e_ref, nused_ref, tok_cur_ref, tok_nxt_ref, h_hbm, gate_ref, wgu_ref, bgu_ref,
                    wd_ref, bd_ref,
                    y_ref,
                    xbuf_ref, sems):
    i = pl.program_id(0)
    nused = nused_ref[0]
    slot = i % 2

    @pl.when(i == 0)
    def _():
        _gather_rows(tok_cur_ref, h_hbm, xbuf_ref.at[0], sems.at[0], EXPERT_ROWS)

    @pl.when(i + 1 < nused)
    def _():
        _gather_rows(tok_nxt_ref, h_hbm, xbuf_ref.at[1 - slot], sems.at[1 - slot], EXPERT_ROWS)

    @pl.when(i < nused)
    def _():
        _gather_wait(h_hbm, xbuf_ref.at[slot], sems.at[slot], EXPERT_ROWS)
        xb = xbuf_ref[slot].astype(BF16)
        gu = _dot(xb, wgu_ref[0]) + bgu_ref[0]
        gate = jnp.minimum(gu[:, :D_MODEL], SWIGLU_LIMIT)
        up = jnp.clip(gu[:, D_MODEL:], -SWIGLU_LIMIT, SWIGLU_LIMIT)
        hid = (up + 1.0) * gate * jax.nn.sigmoid(SWIGLU_ALPHA * gate)
        y = _dot(hid.astype(BF16), wd_ref[0]) + bd_ref[0]
        y_ref[...] = y * gate_ref[...]

    @pl.when(i >= nused)
    def _():
        y_ref[...] = jnp.zeros(y_ref.shape, F32)


def _experts_call(blk_e, nused, row_tok3, h1, row_gate, wgu, bgu, wd, bd):
    nb = row_tok3.shape[0]
    rows = EXPERT_ROWS
    grid_spec = pltpu.PrefetchScalarGridSpec(
        num_scalar_prefetch=2,
        grid=(nb,),
        in_specs=[
            pl.BlockSpec((1, 1, rows), lambda i, be, nu: (i, 0, 0), memory_space=pltpu.SMEM),
            pl.BlockSpec((1, 1, rows), lambda i, be, nu: (jnp.minimum(i + 1, nb - 1), 0, 0),
                         memory_space=pltpu.SMEM),
            pl.BlockSpec(memory_space=pl.ANY),
            pl.BlockSpec((rows, 1), lambda i, be, nu: (i, 0)),
            pl.BlockSpec((1, D_MODEL, 2 * D_MODEL), lambda i, be, nu: (be[i], 0, 0)),
            pl.BlockSpec((1, 1, 2 * D_MODEL), lambda i, be, nu: (be[i], 0, 0)),
            pl.BlockSpec((1, D_MODEL, D_MODEL), lambda i, be, nu: (be[i], 0, 0)),
            pl.BlockSpec((1, 1, D_MODEL), lambda i, be, nu: (be[i], 0, 0)),
        ],
        out_specs=pl.BlockSpec((rows, D_MODEL), lambda i, be, nu: (i, 0)),
        scratch_shapes=[pltpu.VMEM((2, rows, D_MODEL), F32), pltpu.SemaphoreType.DMA((2,))],
    )
    return pl.pallas_call(
        _experts_kernel,
        grid_spec=grid_spec,
        out_shape=jax.ShapeDtypeStruct((nb * rows, D_MODEL), F32),
        compiler_params=pltpu.CompilerParams(dimension_semantics=("arbitrary",),
                                             vmem_limit_bytes=VMEM_LIMIT),
        name="experts",
    )(blk_e, nused, row_tok3, row_tok3, h1, row_gate, wgu, bgu, wd, bd)


def _combine_kernel(pos_cur_ref, pos_nxt_ref, ys_hbm, h_ref, g_ref, b_ref,
                    out_ref,
                    ybuf_ref, sems):
    i = pl.program_id(0)
    n = pl.num_programs(0)
    slot = i % 2
    rows = TOP_EXPERTS * COMBINE_ROWS

    @pl.when(i == 0)
    def _():
        _gather_rows(pos_cur_ref, ys_hbm, ybuf_ref.at[0], sems.at[0], rows)

    @pl.when(i + 1 < n)
    def _():
        _gather_rows(pos_nxt_ref, ys_hbm, ybuf_ref.at[1 - slot], sems.at[1 - slot], rows)

    _gather_wait(ys_hbm, ybuf_ref.at[slot], sems.at[slot], rows)
    ffn = ybuf_ref[slot, 0:COMBINE_ROWS, :]
    for j in range(1, TOP_EXPERTS):
        ffn = ffn + ybuf_ref[slot, j * COMBINE_ROWS:(j + 1) * COMBINE_ROWS, :]
    out_ref[...] = _layer_norm(DN_ALPHA * h_ref[...] + ffn, g_ref[...], b_ref[...])


def _combine_call(pos3, ys, h1, g, b):
    t = h1.shape[0]
    rows = COMBINE_ROWS
    nsteps = t // rows
    full = lambda a: pl.BlockSpec(a.shape, lambda i: (0,) * a.ndim)
    return pl.pallas_call(
        _combine_kernel,
        grid=(nsteps,),
        in_specs=[
            pl.BlockSpec((1, 1, TOP_EXPERTS * rows), lambda i: (i, 0, 0), memory_space=pltpu.SMEM),
            pl.BlockSpec((1, 1, TOP_EXPERTS * rows), lambda i: (jnp.minimum(i + 1, nsteps - 1), 0, 0),
                         memory_space=pltpu.SMEM),
            pl.BlockSpec(memory_space=pl.ANY),
            pl.BlockSpec((rows, D_MODEL), lambda i: (i, 0)),
            full(g), full(b),
        ],
        out_specs=pl.BlockSpec((rows, D_MODEL), lambda i: (i, 0)),
        out_shape=jax.ShapeDtypeStruct((t, D_MODEL), F32),
        scratch_shapes=[pltpu.VMEM((2, TOP_EXPERTS * rows, D_MODEL), F32), pltpu.SemaphoreType.DMA((2,))],
        compiler_params=pltpu.CompilerParams(dimension_semantics=("arbitrary",),
                                             vmem_limit_bytes=VMEM_LIMIT),
        name="combine",
    )(pos3, pos3, ys, h1, g, b)


def _routing_tables(top_e, gates):
    t = top_e.shape[0]
    a = t * TOP_EXPERTS
    rows = EXPERT_ROWS
    flat_e = top_e.reshape(a)
    order = jnp.argsort(flat_e, stable=True).astype(I32)
    counts = jnp.sum((flat_e[:, None] == jnp.arange(N_EXPERTS, dtype=I32)[None, :]).astype(I32), axis=0)
    padded = (counts + rows - 1) // rows * rows
    start = jnp.cumsum(counts) - counts
    pad_end = jnp.cumsum(padded)
    pad_start = pad_end - padded
    nb = a // rows + N_EXPERTS
    nused = (pad_end[-1] // rows).astype(I32)
    blk_e = jnp.minimum(jnp.searchsorted(pad_end, jnp.arange(nb, dtype=I32) * rows, side="right"),
                        N_EXPERTS - 1).astype(I32)
    p = jnp.arange(nb * rows, dtype=I32)
    pe = blk_e[p // rows]
    within = p - pad_start[pe]
    valid = within < counts[pe]
    src = order[jnp.clip(start[pe] + within, 0, a - 1)]
    row_tok = jnp.where(valid, src // TOP_EXPERTS, 0).astype(I32)
    row_gate = jnp.where(valid, gates.reshape(a)[src], 0.0).astype(F32)
    rank = jnp.zeros((a,), I32).at[order].set(jnp.arange(a, dtype=I32))
    dest = pad_start[flat_e] + rank - start[flat_e]
    return blk_e, nused.reshape(1), row_tok, row_gate, dest


def kernel(x, w_in, kv_norm_g, w_uk, w_uv, m_conv_w, m_conv_b, w_mq, w_mk, b_i, b_f, m_norm_g, w_out,
           ln1_g, ln1_b, w_router, b_router, w_gu, b_gu, w_down, b_down, ln2_g, ln2_b):
    bsz, seq, _ = x.shape
    assert seq == SEQ
    t = bsz * seq
    x2 = x.reshape(t, D_MODEL)
    w = w_in[0]
    o_qa, o_ckv, o_qi, o_ki, o_wi, o_u, o_v, o_z, o_i, o_f, o_end = 0, 512, 640, 1152, 1216, 1224, 1736, 2248, 2760, 2764, 2768
    wmain = jnp.concatenate([w[:, o_qa:o_ki], w[:, o_u:o_i]], axis=1).astype(BF16)
    wsmall = jnp.zeros((D_MODEL, LANES), F32).at[:, 0:8].set(w[:, o_wi:o_u]).at[:, 8:16].set(w[:, o_i:o_end]).astype(BF16)
    wsmallt = jnp.zeros((LANES, D_MODEL), F32).at[0:64].set(w[:, o_ki:o_wi].T).at[64:72].set(w[:, o_i:o_end].T).astype(BF16)
    wukt = jnp.swapaxes(w_uk[0], 1, 2).astype(BF16)

    qlat, c, qidx, small, smallt, qm, km, v, z = _proj_call(
        x2, wmain, wsmall, wsmallt, wukt, kv_norm_g[0].reshape(1, KV_RANK), m_conv_w[0],
        m_conv_b[0].reshape(1, M_WIDTH), w_mq[0].astype(BF16), w_mk[0].astype(BF16))

    qi = qidx.reshape(bsz, N_CHUNKS, CHUNK, IDX_HEADS, IDX_DIM).transpose(0, 1, 3, 2, 4)
    qi = qi.reshape(bsz, N_CHUNKS, IDX_HEADS * CHUNK, IDX_DIM)
    wcol = small[:, 0:8].reshape(bsz, N_CHUNKS, CHUNK, IDX_HEADS).transpose(0, 1, 3, 2)
    wcol = wcol.reshape(bsz, N_CHUNKS, IDX_HEADS * CHUNK, 1)
    slopes = np.repeat(2.0 ** (-8.0 * np.arange(1, A_HEADS + 1) / A_HEADS), CHUNK).astype(np.float32)
    aout = _dsa_call(qi, wcol, jnp.asarray(slopes).reshape(A_HEADS * CHUNK, 1),
                     qlat.reshape(bsz, seq, A_HEADS * KV_RANK), smallt,
                     c.reshape(bsz, seq, KV_RANK), w_uv[0].astype(BF16))

    grow = smallt[64:72].reshape(8, bsz * N_CHUNKS, CHUNK).transpose(1, 0, 2)
    bias8 = jnp.concatenate([b_i[0], b_f[0]])
    bcol = jnp.zeros((1, LANES), F32).at[0, 8:16].set(bias8)
    mout = _mlstm_call(qm.reshape(bsz, seq, M_WIDTH), km.reshape(bsz, seq, M_WIDTH),
                       v.reshape(bsz, seq, M_WIDTH), z.reshape(bsz, seq, M_WIDTH),
                       small.reshape(bsz, seq, LANES), grow, bcol, bias8.reshape(8, 1),
                       m_norm_g[0].reshape(1, M_WIDTH))

    wr = jnp.zeros((D_MODEL, LANES), F32).at[:, :N_EXPERTS].set(w_router[0]).astype(BF16)
    br = jnp.full((1, LANES), NEG_BIG, F32).at[0, :N_EXPERTS].set(b_router[0])
    h1, e_slab, g_slab = _mix_call(aout.reshape(t, A_WIDTH), mout.reshape(t, M_WIDTH), x2,
                                   w_out[0].astype(BF16), ln1_g[0].reshape(1, D_MODEL),
                                   ln1_b[0].reshape(1, D_MODEL), wr, br)

    blk_e, nused, row_tok, row_gate, dest = _routing_tables(e_slab[:, :TOP_EXPERTS], g_slab[:, :TOP_EXPERTS])
    nb = blk_e.shape[0]
    ys = _experts_call(blk_e, nused, row_tok.reshape(nb, 1, EXPERT_ROWS), h1,
                       row_gate.reshape(nb * EXPERT_ROWS, 1), w_gu[0].astype(BF16),
                       b_gu[0].reshape(N_EXPERTS, 1, 2 * D_MODEL), w_down[0].astype(BF16),
                       b_down[0].reshape(N_EXPERTS, 1, D_MODEL))
    pos = dest.reshape(t // COMBINE_ROWS, COMBINE_ROWS, TOP_EXPERTS).transpose(0, 2, 1)
    pos = pos.reshape(t // COMBINE_ROWS, 1, TOP_EXPERTS * COMBINE_ROWS)
    out = _combine_call(pos, ys, h1, ln2_g[0].reshape(1, D_MODEL), ln2_b[0].reshape(1, D_MODEL))
    return out.reshape(bsz, seq, D_MODEL)
```

```python
import functools

import jax
import jax.numpy as jnp
import numpy as np
from jax import lax
from jax.experimental import pallas as pl
from jax.experimental.pallas import tpu as pltpu

F32 = jnp.float32
BF16 = jnp.bfloat16
I32 = jnp.int32

D_MODEL = 1024
SEQ = 2048
CHUNK = 64
A_HEADS = 8
A_HEAD_DIM = 64
KV_RANK = 128
IDX_HEADS = 8
IDX_DIM = 64
K_SEL = 256
M_HEADS = 4
M_HEAD_DIM = 128
CONV_WIDTH = 4
N_EXPERTS = 32
TOP_EXPERTS = 4
SWIGLU_LIMIT = 7.0
SWIGLU_ALPHA = 1.702
LN_EPS = 1e-5
DN_ALPHA = 2.0 ** 0.25

A_WIDTH = A_HEADS * A_HEAD_DIM
M_WIDTH = M_HEADS * M_HEAD_DIM
N_CHUNKS = SEQ // CHUNK

LANES = 128
NEG_BIG = -1e30
INT_MIN = -(2 ** 31)

PROJ_ROWS = 512
KEY_BLOCK = 256
EXPERT_ROWS = 256
COMBINE_ROWS = 256
VMEM_LIMIT = 56 * 1024 * 1024

_QA, _CKV, _QI, _U, _V, _Z = 0, 512, 640, 1152, 1664, 2176
MAIN_WIDTH = 2688


def _dot(a, b):
    return jnp.dot(a, b, preferred_element_type=F32)


def _dot_nt(a, b):
    return lax.dot_general(a, b, (((1,), (1,)), ((), ())), preferred_element_type=F32)


def _proj_kernel(x_ref, wmain_ref, wsmall_ref, wsmallt_ref, wukt_ref, kvg_ref, convw_ref, convb_ref,
                 wmq_ref, wmk_ref,
                 qlat_ref, c_ref, qidx_ref, small_ref, smallt_ref, qm_ref, km_ref, v_ref, z_ref,
                 uext_ref):
    i = pl.program_id(0)
    rows = x_ref.shape[0]
    xb = x_ref[...].astype(BF16)

    qa = _dot(xb, wmain_ref[:, _QA:_CKV])
    for h in range(A_HEADS):
        qh = qa[:, h * A_HEAD_DIM:(h + 1) * A_HEAD_DIM].astype(BF16)
        ql = _dot(qh, wukt_ref[h]) * (A_HEAD_DIM ** -0.5)
        qlat_ref[:, h * KV_RANK:(h + 1) * KV_RANK] = ql.astype(BF16)

    ckv = _dot(xb, wmain_ref[:, _CKV:_QI])
    ms = jnp.mean(ckv * ckv, axis=-1, keepdims=True)
    c_ref[...] = (ckv * lax.rsqrt(ms + LN_EPS) * kvg_ref[...]).astype(BF16)

    qidx_ref[...] = _dot(xb, wmain_ref[:, _QI:_U]).astype(BF16)
    v_ref[...] = _dot(xb, wmain_ref[:, _V:_Z]).astype(BF16)
    z_ref[...] = _dot(xb, wmain_ref[:, _Z:MAIN_WIDTH])
    small_ref[...] = _dot(xb, wsmall_ref[...])
    smallt_ref[...] = _dot_nt(wsmallt_ref[...], xb)

    u = _dot(xb, wmain_ref[:, _U:_V])

    @pl.when(i % (SEQ // rows) == 0)
    def _():
        uext_ref[0:8, :] = jnp.zeros((8, M_WIDTH), F32)

    uext_ref[8:8 + rows, :] = u
    acc = jnp.broadcast_to(convb_ref[...], (rows, M_WIDTH))
    for j in range(CONV_WIDTH):
        off = 8 - (CONV_WIDTH - 1) + j
        acc = acc + convw_ref[j:j + 1, :] * uext_ref[off:off + rows, :]
    uext_ref[0:8, :] = u[rows - 8:rows, :]
    uc = acc * jax.nn.sigmoid(acc)
    for h in range(M_HEADS):
        sl = slice(h * M_HEAD_DIM, (h + 1) * M_HEAD_DIM)
        uh = uc[:, sl].astype(BF16)
        qm_ref[:, sl] = _dot(uh, wmq_ref[h]).astype(BF16)
        km_ref[:, sl] = (_dot(uh, wmk_ref[h]) * (M_HEAD_DIM ** -0.5)).astype(BF16)


def _proj_call(x2, wmain, wsmall, wsmallt, wukt, kvg, convw, convb, wmq, wmk):
    t = x2.shape[0]
    rows = PROJ_ROWS
    full = lambda a: pl.BlockSpec(a.shape, lambda i: (0,) * a.ndim)
    tok = lambda w: pl.BlockSpec((rows, w), lambda i: (i, 0))
    out_shape = (
        jax.ShapeDtypeStruct((t, A_HEADS * KV_RANK), BF16),
        jax.ShapeDtypeStruct((t, KV_RANK), BF16),
        jax.ShapeDtypeStruct((t, IDX_HEADS * IDX_DIM), BF16),
        jax.ShapeDtypeStruct((t, LANES), F32),
        jax.ShapeDtypeStruct((LANES, t), F32),
        jax.ShapeDtypeStruct((t, M_WIDTH), BF16),
        jax.ShapeDtypeStruct((t, M_WIDTH), BF16),
        jax.ShapeDtypeStruct((t, M_WIDTH), BF16),
        jax.ShapeDtypeStruct((t, M_WIDTH), F32),
    )
    out_specs = (tok(A_HEADS * KV_RANK), tok(KV_RANK), tok(IDX_HEADS * IDX_DIM), tok(LANES),
                 pl.BlockSpec((LANES, rows), lambda i: (0, i)),
                 tok(M_WIDTH), tok(M_WIDTH), tok(M_WIDTH), tok(M_WIDTH))
    return pl.pallas_call(
        _proj_kernel,
        grid=(t // rows,),
        in_specs=[tok(D_MODEL), full(wmain), full(wsmall), full(wsmallt), full(wukt), full(kvg),
                  full(convw), full(convb), full(wmq), full(wmk)],
        out_specs=out_specs,
        out_shape=out_shape,
        scratch_shapes=[pltpu.VMEM((8 + rows, M_WIDTH), F32)],
        compiler_params=pltpu.CompilerParams(dimension_semantics=("arbitrary",),
                                             vmem_limit_bytes=VMEM_LIMIT),
        name="proj",
    )(x2, wmain, wsmall, wsmallt, wukt, kvg, convw, convb, wmq, wmk)


def _count_rows(keys_ref, nkb, pred):
    def body(kb, acc):
        key = keys_ref[kb]
        pos = kb * KEY_BLOCK + lax.broadcasted_iota(I32, (CHUNK, KEY_BLOCK), 1)
        return acc + jnp.where(pred(key, pos), 1.0, 0.0)
    acc = lax.fori_loop(0, nkb, body, jnp.zeros((CHUNK, KEY_BLOCK), F32))
    return jnp.sum(acc, axis=-1, keepdims=True)


def _dsa_kernel(qi_ref, wcol_ref, slope_ref, qlat_ref, kidxt_ref, c_ref, wuv_ref,
                aout_ref,
                kt_ref, keys_ref, thr_ref, jlim_ref):
    ci = pl.program_id(1)
    hrows = A_HEADS * CHUNK

    @pl.when(ci == 0)
    def _():
        for kb in range(SEQ // KEY_BLOCK):
            kt_ref[kb] = kidxt_ref[:, kb * KEY_BLOCK:(kb + 1) * KEY_BLOCK].astype(BF16)

    nkb = ci // (KEY_BLOCK // CHUNK) + 1
    nkeys = (ci + 1) * CHUNK
    qi = qi_ref[0, 0]
    wcol = wcol_ref[0, 0]

    def score_body(kb, carry):
        rel = jnp.maximum(_dot(qi, kt_ref[kb]), 0.0) * wcol
        sc = rel[0:CHUNK]
        for h in range(1, IDX_HEADS):
            sc = sc + rel[h * CHUNK:(h + 1) * CHUNK]
        pos = kb * KEY_BLOCK + lax.broadcasted_iota(I32, (CHUNK, KEY_BLOCK), 1)
        keys_ref[kb] = jnp.where(pos < nkeys, sc, -jnp.inf)
        return carry

    lax.fori_loop(0, nkb, score_body, 0)

    thr_ref[...] = jnp.full(thr_ref.shape, -jnp.inf, F32)
    jlim_ref[...] = jnp.full(jlim_ref.shape, -1, I32)

    def as_score(code):
        return pltpu.bitcast(jnp.where(code < 0, code ^ jnp.int32(0x7FFFFFFF), code), F32)

    @pl.when(nkeys > K_SEL)
    def _():
        k_sel = jnp.float32(K_SEL)
        cnt0 = _count_rows(keys_ref, nkb, lambda key, pos: key >= 0.0)
        t0 = jnp.where(cnt0 >= k_sel, jnp.int32(0), jnp.int32(INT_MIN))

        def bit_body(p, t):
            cand = t + lax.shift_left(jnp.int32(1), jnp.int32(30) - p)
            cand_f = as_score(cand)
            cnt = _count_rows(keys_ref, nkb, lambda key, pos: key >= cand_f)
            return jnp.where(cnt >= k_sel, cand, t)

        t = as_score(lax.fori_loop(0, 31, bit_body, t0))
        thr_ref[...] = jnp.broadcast_to(t, thr_ref.shape)
        jlim_ref[...] = jnp.full(jlim_ref.shape, SEQ - 1, I32)
        cnt_ge = _count_rows(keys_ref, nkb, lambda key, pos: key >= t)

        @pl.when(jnp.max(cnt_ge) > k_sel)
        def _():
            cnt_gt = _count_rows(keys_ref, nkb, lambda key, pos: key > t)
            need = k_sel - cnt_gt

            def pos_body(p, j):
                cand = j + lax.shift_left(jnp.int32(1), jnp.int32(10) - p)
                cnt = _count_rows(keys_ref, nkb, lambda key, pos: (key == t) & (pos < cand))
                return jnp.where(cnt < need, cand, j)

            j = lax.fori_loop(0, 11, pos_body, jnp.zeros((CHUNK, 1), I32))
            jlim_ref[...] = jnp.broadcast_to(j, jlim_ref.shape)

    thr = thr_ref[:, 0:1]
    jlim = jlim_ref[:, 0:1]
    qlat = jnp.concatenate(
        [qlat_ref[0, :, h * KV_RANK:(h + 1) * KV_RANK] for h in range(A_HEADS)], axis=0)
    slope = slope_ref[...].reshape(A_HEADS, CHUNK, 1)
    qpos = ci * CHUNK + lax.broadcasted_iota(I32, (CHUNK, KEY_BLOCK), 0)

    def att_body(kb, carry):
        m, l, acc = carry
        off = pl.multiple_of(kb * KEY_BLOCK, KEY_BLOCK)
        cblk = c_ref[0, pl.ds(off, KEY_BLOCK), :]
        s = _dot_nt(qlat, cblk)
        key = keys_ref[kb]
        pos = off + lax.broadcasted_iota(I32, (CHUNK, KEY_BLOCK), 1)
        sel = (key > thr) | ((key == thr) & (pos <= jlim))
        bias = jnp.where(sel, 0.0, NEG_BIG)
        dist = jnp.abs(qpos - pos).astype(F32)
        s = (s.reshape(A_HEADS, CHUNK, KEY_BLOCK) - slope * dist[None] + bias[None]).reshape(hrows, KEY_BLOCK)
        m_new = jnp.maximum(m, jnp.max(s, axis=-1, keepdims=True))
        alpha = jnp.exp(m - m_new)
        p = jnp.exp(s - m_new)
        l_new = alpha * l + jnp.sum(p, axis=-1, keepdims=True)
        acc_new = alpha * acc + _dot(p.astype(BF16), cblk)
        return m_new, l_new, acc_new

    init = (jnp.full((hrows, 1), NEG_BIG, F32), jnp.zeros((hrows, 1), F32),
            jnp.zeros((hrows, KV_RANK), F32))
    _, l, acc = lax.fori_loop(0, nkb, att_body, init)
    o = (acc / l).astype(BF16)
    outs = [_dot(o[h * CHUNK:(h + 1) * CHUNK], wuv_ref[h]) for h in range(A_HEADS)]
    aout_ref[0] = jnp.concatenate(outs, axis=-1).astype(BF16)


def _dsa_call(qi, wcol, slope, qlat, smallt, c, wuv):
    b = qlat.shape[0]
    return pl.pallas_call(
        _dsa_kernel,
        grid=(b, N_CHUNKS),
        in_specs=[
            pl.BlockSpec((1, 1, A_HEADS * CHUNK, IDX_DIM), lambda bi, ci: (bi, ci, 0, 0)),
            pl.BlockSpec((1, 1, A_HEADS * CHUNK, 1), lambda bi, ci: (bi, ci, 0, 0)),
            pl.BlockSpec((A_HEADS * CHUNK, 1), lambda bi, ci: (0, 0)),
            pl.BlockSpec((1, CHUNK, A_HEADS * KV_RANK), lambda bi, ci: (bi, ci, 0)),
            pl.BlockSpec((IDX_DIM, SEQ), lambda bi, ci: (0, bi)),
            pl.BlockSpec((1, SEQ, KV_RANK), lambda bi, ci: (bi, 0, 0)),
            pl.BlockSpec(wuv.shape, lambda bi, ci: (0, 0, 0)),
        ],
        out_specs=pl.BlockSpec((1, CHUNK, A_WIDTH), lambda bi, ci: (bi, ci, 0)),
        out_shape=jax.ShapeDtypeStruct((b, SEQ, A_WIDTH), BF16),
        scratch_shapes=[pltpu.VMEM((SEQ // KEY_BLOCK, IDX_DIM, KEY_BLOCK), BF16),
                        pltpu.VMEM((SEQ // KEY_BLOCK, CHUNK, KEY_BLOCK), F32),
                        pltpu.VMEM((CHUNK, LANES), F32), pltpu.VMEM((CHUNK, LANES), I32)],
        compiler_params=pltpu.CompilerParams(dimension_semantics=("arbitrary", "arbitrary"),
                                             vmem_limit_bytes=VMEM_LIMIT),
        name="dsa",
    )(qi, wcol, slope, qlat, smallt, c, wuv)


def _log_sigmoid(x):
    return jnp.minimum(x, 0.0) - jnp.log(1.0 + jnp.exp(-jnp.abs(x)))


def _mlstm_kernel(qm_ref, km_ref, v_ref, z_ref, gcol_ref, grow_ref, bcol_ref, brow_ref, normg_ref,
                  out_ref,
                  ct_ref, n_ref, m_ref):
    ci = pl.program_id(1)

    @pl.when(ci == 0)
    def _():
        ct_ref[...] = jnp.zeros(ct_ref.shape, F32)
        n_ref[...] = jnp.zeros(n_ref.shape, F32)
        m_ref[...] = jnp.full(m_ref.shape, NEG_BIG, F32)

    hi = lax.Precision.HIGHEST
    gcol = gcol_ref[0] + bcol_ref[...]
    grow = grow_ref[0] + brow_ref[...]
    r_i = lax.broadcasted_iota(I32, (CHUNK, CHUNK), 0)
    c_i = lax.broadcasted_iota(I32, (CHUNK, CHUNK), 1)
    causal = c_i <= r_i
    tril = jnp.where(causal, 1.0, 0.0).astype(F32)
    triu = jnp.where(r_i <= c_i, 1.0, 0.0).astype(F32)
    lf_col = _log_sigmoid(gcol)
    lf_row = _log_sigmoid(grow)
    bcum_col = jnp.dot(tril, lf_col, precision=hi, preferred_element_type=F32)
    bcum_row = jnp.dot(lf_row, triu, precision=hi, preferred_element_type=F32)

    for h in range(M_HEADS):
        sl = slice(h * M_HEAD_DIM, (h + 1) * M_HEAD_DIM)
        q = qm_ref[0, :, sl]
        k = km_ref[0, :, sl]
        v = v_ref[0, :, sl]
        li_col = gcol[:, 8 + h:9 + h]
        li_row = grow[h:h + 1, :]
        b_col = bcum_col[:, 12 + h:13 + h]
        b_row = bcum_row[4 + h:5 + h, :]
        b_tot = b_row[:, CHUNK - 1:CHUNK]
        ct_prev = ct_ref[h]
        n_prev = n_ref[h]
        m_prev = m_ref[h][:, 0:1]

        g = b_col + m_prev
        dlog = jnp.where(causal, b_col - b_row + li_row, NEG_BIG)
        m_t = jnp.maximum(g, jnp.max(dlog, axis=-1, keepdims=True))
        w_intra = jnp.exp(dlog - m_t)
        w_state = jnp.exp(g - m_t)
        s = _dot_nt(q, k) * w_intra
        num = _dot(s.astype(BF16), v) + w_state * _dot(q, ct_prev.astype(BF16))
        den = (jnp.sum(s, axis=-1, keepdims=True)
               + w_state * jnp.sum(q.astype(F32) * n_prev, axis=-1, keepdims=True))
        hc = num / jnp.maximum(jnp.abs(den), jnp.exp(-m_t))
        mu = jnp.mean(hc, axis=-1, keepdims=True)
        var = jnp.mean(jnp.square(hc - mu), axis=-1, keepdims=True)
        hc = (hc - mu) * lax.rsqrt(var + LN_EPS)
        out_ref[0, :, sl] = (jax.nn.sigmoid(z_ref[0, :, sl]) * (hc * normg_ref[:, sl])).astype(BF16)

        a_col = b_tot - b_col + li_col
        a_max = jnp.max(a_col, axis=0, keepdims=True)
        m_new = jnp.maximum(b_tot + m_prev, a_max)
        decay = jnp.exp(b_tot + m_prev - m_new)
        kw = k.astype(F32) * jnp.exp(a_col - m_new)
        ct_ref[h] = decay * ct_prev + _dot(kw.T.astype(BF16), v)
        n_ref[h] = decay * n_prev + jnp.sum(kw, axis=0, keepdims=True)
        m_ref[h] = jnp.broadcast_to(m_new, m_ref.shape[1:])


def _mlstm_call(qm, km, v, z, gcol, grow, bcol, brow, normg):
    b = qm.shape[0]
    tokspec = lambda w: pl.BlockSpec((1, CHUNK, w), lambda bi, ci: (bi, ci, 0))
    full = lambda a: pl.BlockSpec(a.shape, lambda bi, ci: (0,) * a.ndim)
    return pl.pallas_call(
        _mlstm_kernel,
        grid=(b, N_CHUNKS),
        in_specs=[tokspec(M_WIDTH), tokspec(M_WIDTH), tokspec(M_WIDTH), tokspec(M_WIDTH), tokspec(LANES),
                  pl.BlockSpec((1, 8, CHUNK), lambda bi, ci: (bi * N_CHUNKS + ci, 0, 0)),
                  full(bcol), full(brow), full(normg)],
        out_specs=tokspec(M_WIDTH),
        out_shape=jax.ShapeDtypeStruct((b, SEQ, M_WIDTH), BF16),
        scratch_shapes=[pltpu.VMEM((M_HEADS, M_HEAD_DIM, M_HEAD_DIM), F32),
                        pltpu.VMEM((M_HEADS, 1, M_HEAD_DIM), F32),
                        pltpu.VMEM((M_HEADS, 1, LANES), F32)],
        compiler_params=pltpu.CompilerParams(dimension_semantics=("arbitrary", "arbitrary"),
                                             vmem_limit_bytes=VMEM_LIMIT),
        name="mlstm",
    )(qm, km, v, z, gcol, grow, bcol, brow, normg)


def _layer_norm(y, g, b):
    mu = jnp.mean(y, axis=-1, keepdims=True)
    var = jnp.mean(jnp.square(y - mu), axis=-1, keepdims=True)
    return (y - mu) * lax.rsqrt(var + LN_EPS) * g + b


def _to_row_tiles(ref, val):
    n = val.shape[0]
    for k in range(D_MODEL // LANES):
        ref[pl.ds(k, n, stride=8), :] = val[:, k * LANES:(k + 1) * LANES]


def _from_row_tiles(ref, n):
    return jnp.concatenate([ref[pl.ds(k, n, stride=8), :] for k in range(D_MODEL // LANES)], axis=1)


def _mix_kernel(a_ref, m_ref, x_ref, wout_ref, g_ref, b_ref, wr_ref, br_ref,
                h_ref, ht_ref, e_ref, gate_ref, cnt_ref):
    @pl.when(pl.program_id(0) == 0)
    def _():
        cnt_ref[...] = jnp.zeros(cnt_ref.shape, F32)

    mix = _dot(a_ref[...], wout_ref[0:A_WIDTH, :]) + _dot(m_ref[...], wout_ref[A_WIDTH:, :])
    h1 = _layer_norm(DN_ALPHA * x_ref[...] + mix, g_ref[...], b_ref[...])
    h_ref[...] = h1
    _to_row_tiles(ht_ref, h1)
    logits = _dot(h1.astype(BF16), wr_ref[...]) + br_ref[...]
    lane = lax.broadcasted_iota(I32, logits.shape, 1).astype(F32)
    e_slab = jnp.zeros(logits.shape, F32)
    v_slab = jnp.zeros(logits.shape, F32)
    vals = []
    chosen = jnp.zeros(logits.shape, F32)
    for j in range(TOP_EXPERTS):
        mx = jnp.max(logits, axis=-1, keepdims=True)
        idx = jnp.min(jnp.where(logits == mx, lane, float(LANES)), axis=-1, keepdims=True)
        e_slab = jnp.where(lane == float(j), idx, e_slab)
        vals.append(mx)
        chosen = jnp.where(lane == idx, 1.0, chosen)
        logits = jnp.where(lane == idx, -jnp.inf, logits)
    cnt_ref[...] += jnp.sum(chosen, axis=0, keepdims=True)
    ex = [jnp.exp(vj - vals[0]) for vj in vals]
    tot = ex[0] + ex[1] + ex[2] + ex[3]
    for j in range(TOP_EXPERTS):
        v_slab = jnp.where(lane == float(j), ex[j] / tot, v_slab)
    e_ref[...] = e_slab.astype(I32)
    gate_ref[...] = v_slab


def _mix_call(aout, mout, x2, wout, g, b, wr, br):
    t = x2.shape[0]
    rows = PROJ_ROWS
    full = lambda a: pl.BlockSpec(a.shape, lambda i: (0,) * a.ndim)
    tok = lambda w: pl.BlockSpec((rows, w), lambda i: (i, 0))
    return pl.pallas_call(
        _mix_kernel,
        grid=(t // rows,),
        in_specs=[tok(A_WIDTH), tok(M_WIDTH), tok(D_MODEL), full(wout), full(g), full(b), full(wr), full(br)],
        out_specs=(tok(D_MODEL), pl.BlockSpec((8 * rows, LANES), lambda i: (i, 0)), tok(LANES), tok(LANES),
                   pl.BlockSpec((1, LANES), lambda i: (0, 0))),
        out_shape=(jax.ShapeDtypeStruct((t, D_MODEL), F32),
                   jax.ShapeDtypeStruct((8 * t, LANES), F32),
                   jax.ShapeDtypeStruct((t, LANES), I32),
                   jax.ShapeDtypeStruct((t, LANES), F32),
                   jax.ShapeDtypeStruct((1, LANES), F32)),
        compiler_params=pltpu.CompilerParams(dimension_semantics=("arbitrary",),
                                             vmem_limit_bytes=VMEM_LIMIT),
        name="mix",
    )(aout, mout, x2, wout, g, b, wr, br)


ASSIGN_BITS = 18
FFN_SECTIONS = 8
HID_CHUNK = D_MODEL // 4


def _experts_kernel(blk_e_ref, nused_ref,
                    prev_ref, cur_ref, next_ref, h_hbm, wgu_ref, bgu_ref, wd_ref, bd_ref,
                    ys_hbm,
                    xbuf0_ref, xbuf1_ref, hbuf_ref, ybuf0_ref, ybuf1_ref, in_sem, out_sem):
    i = pl.program_id(0)
    nused = nused_ref[0]
    rows = EXPERT_ROWS
    n_tok = h_hbm.shape[0] // 8
    per_sec = rows // FFN_SECTIONS
    xbuf = (xbuf0_ref, xbuf1_ref)
    ybuf = (ybuf0_ref, ybuf1_ref)

    def in_wait(s):
        pltpu.make_async_copy(h_hbm.at[pl.ds(0, 8 * rows)], xbuf[s], in_sem).wait()

    def out_wait(s):
        pltpu.make_async_copy(ybuf[s], ys_hbm.at[pl.ds(0, 8 * rows)], out_sem).wait()

    def tile(r):
        return pl.ds(r * 8, 8) if isinstance(r, int) else pl.ds(pl.multiple_of(r * 8, 8), 8)

    def gather_row(a, s, r):
        tok = a & (n_tok - 1)
        pltpu.make_async_copy(h_hbm.at[tile(tok)], xbuf[s].at[tile(r)], in_sem).start()

    def scatter_row(a, s, r):
        pltpu.make_async_copy(ybuf[s].at[tile(r)], ys_hbm.at[tile(a)], out_sem).start()

    @pl.when(i == 0)
    def _():
        ybuf1_ref[...] = jnp.zeros(ybuf1_ref.shape, F32)

        def first(r, carry):
            gather_row(cur_ref[0, 0, r], 0, r)
            return carry
        lax.fori_loop(0, rows, first, 0)

    def moves(s, lo, hi):
        for r in range(lo, hi):
            scatter_row(prev_ref[0, 0, r], 1 - s, r)
            gather_row(next_ref[0, 0, r], 1 - s, r)

    def body(s):
        in_wait(s)
        xb = _from_row_tiles(xbuf[s], rows).astype(BF16)
        sec = 0
        for c in range(D_MODEL // HID_CHUNK):
            lo, hi = c * HID_CHUNK, (c + 1) * HID_CHUNK
            g = _dot(xb, wgu_ref[0, :, lo:hi]) + bgu_ref[0, :, lo:hi]
            u = _dot(xb, wgu_ref[0, :, D_MODEL + lo:D_MODEL + hi]) + bgu_ref[0, :, D_MODEL + lo:D_MODEL + hi]
            g = jnp.minimum(g, SWIGLU_LIMIT)
            u = jnp.clip(u, -SWIGLU_LIMIT, SWIGLU_LIMIT)
            hbuf_ref[:, lo:hi] = ((u + 1.0) * g * jax.nn.sigmoid(SWIGLU_ALPHA * g)).astype(BF16)
            moves(s, sec * per_sec, (sec + 1) * per_sec)
            sec += 1
        hid = hbuf_ref[...]
        for c in range(D_MODEL // HID_CHUNK):
            lo, hi = c * HID_CHUNK, (c + 1) * HID_CHUNK
            y = _dot(hid, wd_ref[0, :, lo:hi]) + bd_ref[0, :, lo:hi]
            for k in range(HID_CHUNK // LANES):
                ybuf[s][pl.ds(lo // LANES + k, rows, stride=8), :] = y[:, k * LANES:(k + 1) * LANES]
            moves(s, sec * per_sec, (sec + 1) * per_sec)
            sec += 1

    for s in range(2):
        @pl.when((i < nused) & (i % 2 == s))
        def _(s=s):
            @pl.when(i >= 1)
            def _():
                out_wait(s)
            body(s)

        @pl.when((i == nused) & (i % 2 == s))
        def _(s=s):
            in_wait(s)
            out_wait(s)

            def last(r, carry):
                scatter_row(prev_ref[0, 0, r], 1 - s, r)
                return carry
            lax.fori_loop(0, rows, last, 0)
            out_wait(1 - s)


def _experts_call(blk_e, nused, row_dst, h1t, wgu, bgu, wd, bd):
    nb = row_dst.shape[0] - 3
    rows = EXPERT_ROWS
    n_assign = h1t.shape[0] // 8 * TOP_EXPERTS
    wspec = lambda shape: pl.BlockSpec(shape, lambda i, be, nu: (be[i], 0, 0))
    dst_spec = lambda d: pl.BlockSpec((1, 1, rows), lambda i, be, nu: (i + d, 0, 0), memory_space=pltpu.SMEM)
    grid_spec = pltpu.PrefetchScalarGridSpec(
        num_scalar_prefetch=2,
        grid=(nb + 1,),
        in_specs=[
            dst_spec(0), dst_spec(1), dst_spec(2),
            pl.BlockSpec(memory_space=pl.ANY),
            wspec((1, D_MODEL, 2 * D_MODEL)), wspec((1, 1, 2 * D_MODEL)),
            wspec((1, D_MODEL, D_MODEL)), wspec((1, 1, D_MODEL)),
        ],
        out_specs=pl.BlockSpec(memory_space=pl.ANY),
        scratch_shapes=[pltpu.VMEM((8 * rows, LANES), F32), pltpu.VMEM((8 * rows, LANES), F32),
                        pltpu.VMEM((rows, D_MODEL), BF16),
                        pltpu.VMEM((8 * rows, LANES), F32), pltpu.VMEM((8 * rows, LANES), F32),
                        pltpu.SemaphoreType.DMA, pltpu.SemaphoreType.DMA],
    )
    return pl.pallas_call(
        _experts_kernel,
        grid_spec=grid_spec,
        out_shape=jax.ShapeDtypeStruct((8 * (n_assign + rows), LANES), F32),
        compiler_params=pltpu.CompilerParams(dimension_semantics=("arbitrary",),
                                             vmem_limit_bytes=VMEM_LIMIT),
        name="experts",
    )(blk_e, nused, row_dst, row_dst, row_dst, h1t, wgu, bgu, wd, bd)


def _combine_kernel(y0_ref, y1_ref, y2_ref, y3_ref, gate_ref, h_ref, g_ref, b_ref, out_ref):
    n = h_ref.shape[0]
    ffn = gate_ref[:, 0:1] * _from_row_tiles(y0_ref, n)
    for j, y_ref in enumerate((y1_ref, y2_ref, y3_ref), start=1):
        ffn = ffn + gate_ref[:, j:j + 1] * _from_row_tiles(y_ref, n)
    out_ref[...] = _layer_norm(DN_ALPHA * h_ref[...] + ffn, g_ref[...], b_ref[...])


def _combine_call(ys, gates, h1, g, b):
    t = h1.shape[0]
    rows = COMBINE_ROWS
    full = lambda a: pl.BlockSpec(a.shape, lambda i: (0,) * a.ndim)
    choice = lambda j: pl.BlockSpec((8 * rows, LANES), lambda i: (j * (t // rows) + i, 0))
    return pl.pallas_call(
        _combine_kernel,
        grid=(t // rows,),
        in_specs=[
            choice(0), choice(1), choice(2), choice(3),
            pl.BlockSpec((rows, LANES), lambda i: (i, 0)),
            pl.BlockSpec((rows, D_MODEL), lambda i: (i, 0)),
            full(g), full(b),
        ],
        out_specs=pl.BlockSpec((rows, D_MODEL), lambda i: (i, 0)),
        out_shape=jax.ShapeDtypeStruct((t, D_MODEL), F32),
        compiler_params=pltpu.CompilerParams(dimension_semantics=("arbitrary",),
                                             vmem_limit_bytes=VMEM_LIMIT),
        name="combine",
    )(ys, ys, ys, ys, gates, h1, g, b)


def _routing_tables(top_e, counts):
    t = top_e.shape[0]
    a = t * TOP_EXPERTS
    rows = EXPERT_ROWS
    assert t & (t - 1) == 0 and a <= (1 << ASSIGN_BITS)
    ids = jnp.arange(a, dtype=I32)
    keys = jnp.sort((top_e.T.reshape(a) << ASSIGN_BITS) | ids)
    nblk = (counts + rows - 1) // rows
    blk_end = jnp.cumsum(nblk)
    start = jnp.cumsum(counts) - counts
    nb = a // rows + N_EXPERTS
    nused = blk_end[-1].astype(I32)
    b = jnp.arange(nb + 2, dtype=I32)
    used = b < nused
    be = jnp.sum((jnp.minimum(b, nused - 1)[:, None] >= blk_end[None, :]).astype(I32), axis=1)
    onehot = (be[:, None] == jnp.arange(N_EXPERTS, dtype=I32)[None, :]).astype(I32)
    pick = lambda v: jnp.sum(onehot * v[None, :], axis=1)
    within = (b - (pick(blk_end) - pick(nblk))) * rows
    blk_start = jnp.where(used, pick(start) + within, 0).astype(I32)
    blk_n = jnp.where(used, jnp.clip(pick(counts) - within, 0, rows), 0).astype(I32)
    r = jnp.arange(rows, dtype=I32)[None, :]
    window = jnp.minimum(blk_start[:, None] + r, a - 1)
    dump = jnp.broadcast_to(a + r, (nb + 2, rows))
    row_dst = jnp.where(r < blk_n[:, None], keys[window] & ((1 << ASSIGN_BITS) - 1), dump)
    row_dst = jnp.concatenate([dump[:1], row_dst], axis=0).reshape(nb + 3, 1, rows)
    return be.astype(I32), nused.reshape(1), row_dst


def kernel(x, w_in, kv_norm_g, w_uk, w_uv, m_conv_w, m_conv_b, w_mq, w_mk, b_i, b_f, m_norm_g, w_out,
           ln1_g, ln1_b, w_router, b_router, w_gu, b_gu, w_down, b_down, ln2_g, ln2_b):
    bsz, seq, _ = x.shape
    assert seq == SEQ
    t = bsz * seq
    x2 = x.reshape(t, D_MODEL)
    w = w_in[0]
    o_qa, o_ckv, o_qi, o_ki, o_wi, o_u, o_v, o_z, o_i, o_f, o_end = 0, 512, 640, 1152, 1216, 1224, 1736, 2248, 2760, 2764, 2768
    wmain = jnp.concatenate([w[:, o_qa:o_ki], w[:, o_u:o_i]], axis=1).astype(BF16)
    wsmall = jnp.zeros((D_MODEL, LANES), F32).at[:, 0:8].set(w[:, o_wi:o_u]).at[:, 8:16].set(w[:, o_i:o_end]).astype(BF16)
    wsmallt = jnp.zeros((LANES, D_MODEL), F32).at[0:64].set(w[:, o_ki:o_wi].T).at[64:72].set(w[:, o_i:o_end].T).astype(BF16)
    wukt = jnp.swapaxes(w_uk[0], 1, 2).astype(BF16)

    qlat, c, qidx, small, smallt, qm, km, v, z = _proj_call(
        x2, wmain, wsmall, wsmallt, wukt, kv_norm_g[0].reshape(1, KV_RANK), m_conv_w[0],
        m_conv_b[0].reshape(1, M_WIDTH), w_mq[0].astype(BF16), w_mk[0].astype(BF16))

    qi = qidx.reshape(bsz, N_CHUNKS, CHUNK, IDX_HEADS, IDX_DIM).transpose(0, 1, 3, 2, 4)
    qi = qi.reshape(bsz, N_CHUNKS, IDX_HEADS * CHUNK, IDX_DIM)
    wcol = small[:, 0:8].reshape(bsz, N_CHUNKS, CHUNK, IDX_HEADS).transpose(0, 1, 3, 2)
    wcol = wcol.reshape(bsz, N_CHUNKS, IDX_HEADS * CHUNK, 1)
    slopes = np.repeat(2.0 ** (-8.0 * np.arange(1, A_HEADS + 1) / A_HEADS), CHUNK).astype(np.float32)
    aout = _dsa_call(qi, wcol, jnp.asarray(slopes).reshape(A_HEADS * CHUNK, 1),
                     qlat.reshape(bsz, seq, A_HEADS * KV_RANK), smallt,
                     c.reshape(bsz, seq, KV_RANK), w_uv[0].astype(BF16))

    grow = smallt[64:72].reshape(8, bsz * N_CHUNKS, CHUNK).transpose(1, 0, 2)
    bias8 = jnp.concatenate([b_i[0], b_f[0]])
    bcol = jnp.zeros((1, LANES), F32).at[0, 8:16].set(bias8)
    mout = _mlstm_call(qm.reshape(bsz, seq, M_WIDTH), km.reshape(bsz, seq, M_WIDTH),
                       v.reshape(bsz, seq, M_WIDTH), z.reshape(bsz, seq, M_WIDTH),
                       small.reshape(bsz, seq, LANES), grow, bcol, bias8.reshape(8, 1),
                       m_norm_g[0].reshape(1, M_WIDTH))

    wr = jnp.zeros((D_MODEL, LANES), F32).at[:, :N_EXPERTS].set(w_router[0]).astype(BF16)
    br = jnp.full((1, LANES), NEG_BIG, F32).at[0, :N_EXPERTS].set(b_router[0])
    h1, h1t, e_slab, g_slab, cnt = _mix_call(aout.reshape(t, A_WIDTH), mout.reshape(t, M_WIDTH), x2,
                                        w_out[0].astype(BF16), ln1_g[0].reshape(1, D_MODEL),
                                        ln1_b[0].reshape(1, D_MODEL), wr, br)

    blk_e, nused, row_dst = _routing_tables(e_slab[:, :TOP_EXPERTS], cnt[0, :N_EXPERTS].astype(I32))
    ys = _experts_call(blk_e, nused, row_dst, h1t, w_gu[0].astype(BF16),
                       b_gu[0].reshape(N_EXPERTS, 1, 2 * D_MODEL), w_down[0].astype(BF16),
                       b_down[0].reshape(N_EXPERTS, 1, D_MODEL))
    out = _combine_call(ys, g_slab, h1, ln2_g[0].reshape(1, D_MODEL), ln2_b[0].reshape(1, D_MODEL))
    return out.reshape(bsz, seq, D_MODEL)
```

```python
import functools

import jax
import jax.numpy as jnp
import numpy as np
from jax import lax
from jax.experimental import pallas as pl
from jax.experimental.pallas import tpu as pltpu

F32 = jnp.float32
BF16 = jnp.bfloat16
I32 = jnp.int32

D_MODEL = 1024
SEQ = 2048
CHUNK = 64
A_HEADS = 8
A_HEAD_DIM = 64
KV_RANK = 128
IDX_HEADS = 8
IDX_DIM = 64
K_SEL = 256
M_HEADS = 4
M_HEAD_DIM = 128
CONV_WIDTH = 4
N_EXPERTS = 32
TOP_EXPERTS = 4
SWIGLU_LIMIT = 7.0
SWIGLU_ALPHA = 1.702
LN_EPS = 1e-5
DN_ALPHA = 2.0 ** 0.25

A_WIDTH = A_HEADS * A_HEAD_DIM
M_WIDTH = M_HEADS * M_HEAD_DIM
N_CHUNKS = SEQ // CHUNK

LANES = 128
NEG_BIG = -1e30
INT_MIN = -(2 ** 31)

PROJ_ROWS = 512
KEY_BLOCK = 256
EXPERT_ROWS = 256
COMBINE_ROWS = 256
VMEM_LIMIT = 56 * 1024 * 1024

_QA, _CKV, _QI, _U, _V, _Z = 0, 512, 640, 1152, 1664, 2176
MAIN_WIDTH = 2688


def _dot(a, b):
    return jnp.dot(a, b, preferred_element_type=F32)


def _dot_nt(a, b):
    return lax.dot_general(a, b, (((1,), (1,)), ((), ())), preferred_element_type=F32)


def _proj_kernel(x_ref, wmain_ref, wsmall_ref, wsmallt_ref, wckvt_ref, wukt_ref, kvg_ref, kvgcol_ref,
                 convw_ref, convb_ref, wmq_ref, wmk_ref,
                 qlat_ref, c_ref, ct_ref, qidx_ref, small_ref, smallt_ref, qm_ref, km_ref, v_ref, z_ref,
                 uext_ref):
    i = pl.program_id(0)
    rows = x_ref.shape[0]
    xb = x_ref[...].astype(BF16)

    qa = _dot(xb, wmain_ref[:, _QA:_CKV])
    for h in range(A_HEADS):
        qh = qa[:, h * A_HEAD_DIM:(h + 1) * A_HEAD_DIM].astype(BF16)
        ql = _dot(qh, wukt_ref[h]) * (A_HEAD_DIM ** -0.5)
        qlat_ref[:, h * KV_RANK:(h + 1) * KV_RANK] = ql.astype(BF16)

    ckv = _dot(xb, wmain_ref[:, _CKV:_QI])
    ms = jnp.mean(ckv * ckv, axis=-1, keepdims=True)
    c_ref[...] = (ckv * lax.rsqrt(ms + LN_EPS) * kvg_ref[...]).astype(BF16)
    ckv_t = _dot_nt(wckvt_ref[...], xb)
    ms_t = jnp.mean(ckv_t * ckv_t, axis=0, keepdims=True)
    ct_ref[...] = (ckv_t * lax.rsqrt(ms_t + LN_EPS) * kvgcol_ref[...]).astype(BF16)

    qidx_ref[...] = _dot(xb, wmain_ref[:, _QI:_U]).astype(BF16)
    v_ref[...] = _dot(xb, wmain_ref[:, _V:_Z]).astype(BF16)
    z_ref[...] = _dot(xb, wmain_ref[:, _Z:MAIN_WIDTH])
    small_ref[...] = _dot(xb, wsmall_ref[...])
    smallt_ref[...] = _dot_nt(wsmallt_ref[...], xb)

    u = _dot(xb, wmain_ref[:, _U:_V])

    @pl.when(i % (SEQ // rows) == 0)
    def _():
        uext_ref[0:8, :] = jnp.zeros((8, M_WIDTH), F32)

    uext_ref[8:8 + rows, :] = u
    acc = jnp.broadcast_to(convb_ref[...], (rows, M_WIDTH))
    for j in range(CONV_WIDTH):
        off = 8 - (CONV_WIDTH - 1) + j
        acc = acc + convw_ref[j:j + 1, :] * uext_ref[off:off + rows, :]
    uext_ref[0:8, :] = u[rows - 8:rows, :]
    uc = acc * jax.nn.sigmoid(acc)
    for h in range(M_HEADS):
        sl = slice(h * M_HEAD_DIM, (h + 1) * M_HEAD_DIM)
        uh = uc[:, sl].astype(BF16)
        qm_ref[:, sl] = _dot(uh, wmq_ref[h]).astype(BF16)
        km_ref[:, sl] = (_dot(uh, wmk_ref[h]) * (M_HEAD_DIM ** -0.5)).astype(BF16)


def _proj_call(x2, wmain, wsmall, wsmallt, wckvt, wukt, kvg, kvgcol, convw, convb, wmq, wmk):
    t = x2.shape[0]
    rows = PROJ_ROWS
    full = lambda a: pl.BlockSpec(a.shape, lambda i: (0,) * a.ndim)
    tok = lambda w: pl.BlockSpec((rows, w), lambda i: (i, 0))
    tok_t = pl.BlockSpec((LANES, rows), lambda i: (0, i))
    out_shape = (
        jax.ShapeDtypeStruct((t, A_HEADS * KV_RANK), BF16),
        jax.ShapeDtypeStruct((t, KV_RANK), BF16),
        jax.ShapeDtypeStruct((KV_RANK, t), BF16),
        jax.ShapeDtypeStruct((t, IDX_HEADS * IDX_DIM), BF16),
        jax.ShapeDtypeStruct((t, LANES), F32),
        jax.ShapeDtypeStruct((LANES, t), F32),
        jax.ShapeDtypeStruct((t, M_WIDTH), BF16),
        jax.ShapeDtypeStruct((t, M_WIDTH), BF16),
        jax.ShapeDtypeStruct((t, M_WIDTH), BF16),
        jax.ShapeDtypeStruct((t, M_WIDTH), F32),
    )
    out_specs = (tok(A_HEADS * KV_RANK), tok(KV_RANK), tok_t, tok(IDX_HEADS * IDX_DIM), tok(LANES), tok_t,
                 tok(M_WIDTH), tok(M_WIDTH), tok(M_WIDTH), tok(M_WIDTH))
    return pl.pallas_call(
        _proj_kernel,
        grid=(t // rows,),
        in_specs=[tok(D_MODEL), full(wmain), full(wsmall), full(wsmallt), full(wckvt), full(wukt), full(kvg),
                  full(kvgcol), full(convw), full(convb), full(wmq), full(wmk)],
        out_specs=out_specs,
        out_shape=out_shape,
        scratch_shapes=[pltpu.VMEM((8 + rows, M_WIDTH), F32)],
        compiler_params=pltpu.CompilerParams(dimension_semantics=("arbitrary",),
                                             vmem_limit_bytes=VMEM_LIMIT),
        name="proj",
    )(x2, wmain, wsmall, wsmallt, wckvt, wukt, kvg, kvgcol, convw, convb, wmq, wmk)


def _count_rows(keys_ref, nkb, pred):
    def body(kb, acc):
        key = keys_ref[kb]
        pos = kb * KEY_BLOCK + lax.broadcasted_iota(I32, (CHUNK, KEY_BLOCK), 1)
        return acc + jnp.where(pred(key, pos), 1.0, 0.0)
    acc = lax.fori_loop(0, nkb, body, jnp.zeros((CHUNK, KEY_BLOCK), F32))
    return jnp.sum(acc, axis=-1, keepdims=True)


def _dsa_kernel(qi_ref, wcol_ref, slope_ref, qlat_ref, kidxt_ref, c_ref, wuv_ref,
                aout_ref,
                kt_ref, keys_ref, thr_ref, jlim_ref):
    ci = pl.program_id(1)
    hrows = A_HEADS * CHUNK

    @pl.when(ci == 0)
    def _():
        for kb in range(SEQ // KEY_BLOCK):
            kt_ref[kb] = kidxt_ref[:, kb * KEY_BLOCK:(kb + 1) * KEY_BLOCK].astype(BF16)

    nkb = ci // (KEY_BLOCK // CHUNK) + 1
    nkeys = (ci + 1) * CHUNK
    qi = qi_ref[0, 0]
    wcol = wcol_ref[0, 0]

    def score_body(kb, carry):
        rel = jnp.maximum(_dot(qi, kt_ref[kb]), 0.0) * wcol
        sc = rel[0:CHUNK]
        for h in range(1, IDX_HEADS):
            sc = sc + rel[h * CHUNK:(h + 1) * CHUNK]
        pos = kb * KEY_BLOCK + lax.broadcasted_iota(I32, (CHUNK, KEY_BLOCK), 1)
        keys_ref[kb] = jnp.where(pos < nkeys, sc, -jnp.inf)
        return carry

    lax.fori_loop(0, nkb, score_body, 0)

    thr_ref[...] = jnp.full(thr_ref.shape, -jnp.inf, F32)
    jlim_ref[...] = jnp.full(jlim_ref.shape, -1, I32)

    def as_score(code):
        return pltpu.bitcast(jnp.where(code < 0, code ^ jnp.int32(0x7FFFFFFF), code), F32)

    @pl.when(nkeys > K_SEL)
    def _():
        k_sel = jnp.float32(K_SEL)
        cnt0 = _count_rows(keys_ref, nkb, lambda key, pos: key >= 0.0)
        t0 = jnp.where(cnt0 >= k_sel, jnp.int32(0), jnp.int32(INT_MIN))

        def bit_body(p, t):
            cand = t + lax.shift_left(jnp.int32(1), jnp.int32(30) - p)
            cand_f = as_score(cand)
            cnt = _count_rows(keys_ref, nkb, lambda key, pos: key >= cand_f)
            return jnp.where(cnt >= k_sel, cand, t)

        t = as_score(lax.fori_loop(0, 31, bit_body, t0))
        thr_ref[...] = jnp.broadcast_to(t, thr_ref.shape)
        jlim_ref[...] = jnp.full(jlim_ref.shape, SEQ - 1, I32)
        cnt_ge = _count_rows(keys_ref, nkb, lambda key, pos: key >= t)

        @pl.when(jnp.max(cnt_ge) > k_sel)
        def _():
            cnt_gt = _count_rows(keys_ref, nkb, lambda key, pos: key > t)
            need = k_sel - cnt_gt

            def pos_body(p, j):
                cand = j + lax.shift_left(jnp.int32(1), jnp.int32(10) - p)
                cnt = _count_rows(keys_ref, nkb, lambda key, pos: (key == t) & (pos < cand))
                return jnp.where(cnt < need, cand, j)

            j = lax.fori_loop(0, 11, pos_body, jnp.zeros((CHUNK, 1), I32))
            jlim_ref[...] = jnp.broadcast_to(j, jlim_ref.shape)

    thr = thr_ref[:, 0:1]
    jlim = jlim_ref[:, 0:1]
    qlat = jnp.concatenate(
        [qlat_ref[0, :, h * KV_RANK:(h + 1) * KV_RANK] for h in range(A_HEADS)], axis=0)
    slope = slope_ref[...].reshape(A_HEADS, CHUNK, 1)
    qpos = ci * CHUNK + lax.broadcasted_iota(I32, (CHUNK, KEY_BLOCK), 0)

    def att_body(kb, carry):
        m, l, acc = carry
        off = pl.multiple_of(kb * KEY_BLOCK, KEY_BLOCK)
        cblk = c_ref[0, pl.ds(off, KEY_BLOCK), :]
        s = _dot_nt(qlat, cblk)
        key = keys_ref[kb]
        pos = off + lax.broadcasted_iota(I32, (CHUNK, KEY_BLOCK), 1)
        sel = (key > thr) | ((key == thr) & (pos <= jlim))
        bias = jnp.where(sel, 0.0, NEG_BIG)
        dist = jnp.abs(qpos - pos).astype(F32)
        s = (s.reshape(A_HEADS, CHUNK, KEY_BLOCK) - slope * dist[None] + bias[None]).reshape(hrows, KEY_BLOCK)
        m_new = jnp.maximum(m, jnp.max(s, axis=-1, keepdims=True))
        alpha = jnp.exp(m - m_new)
        p = jnp.exp(s - m_new)
        l_new = alpha * l + jnp.sum(p, axis=-1, keepdims=True)
        acc_new = alpha * acc + _dot(p.astype(BF16), cblk)
        return m_new, l_new, acc_new

    init = (jnp.full((hrows, 1), NEG_BIG, F32), jnp.zeros((hrows, 1), F32),
            jnp.zeros((hrows, KV_RANK), F32))
    _, l, acc = lax.fori_loop(0, nkb, att_body, init)
    o = (acc / l).astype(BF16)
    outs = [_dot(o[h * CHUNK:(h + 1) * CHUNK], wuv_ref[h]) for h in range(A_HEADS)]
    aout_ref[0] = jnp.concatenate(outs, axis=-1).astype(BF16)


def _dsa_call(qi, wcol, slope, qlat, smallt, c, wuv):
    b = qlat.shape[0]
    return pl.pallas_call(
        _dsa_kernel,
        grid=(b, N_CHUNKS),
        in_specs=[
            pl.BlockSpec((1, 1, A_HEADS * CHUNK, IDX_DIM), lambda bi, ci: (bi, ci, 0, 0)),
            pl.BlockSpec((1, 1, A_HEADS * CHUNK, 1), lambda bi, ci: (bi, ci, 0, 0)),
            pl.BlockSpec((A_HEADS * CHUNK, 1), lambda bi, ci: (0, 0)),
            pl.BlockSpec((1, CHUNK, A_HEADS * KV_RANK), lambda bi, ci: (bi, ci, 0)),
            pl.BlockSpec((IDX_DIM, SEQ), lambda bi, ci: (0, bi)),
            pl.BlockSpec((1, SEQ, KV_RANK), lambda bi, ci: (bi, 0, 0)),
            pl.BlockSpec(wuv.shape, lambda bi, ci: (0, 0, 0)),
        ],
        out_specs=pl.BlockSpec((1, CHUNK, A_WIDTH), lambda bi, ci: (bi, ci, 0)),
        out_shape=jax.ShapeDtypeStruct((b, SEQ, A_WIDTH), BF16),
        scratch_shapes=[pltpu.VMEM((SEQ // KEY_BLOCK, IDX_DIM, KEY_BLOCK), BF16),
                        pltpu.VMEM((SEQ // KEY_BLOCK, CHUNK, KEY_BLOCK), F32),
                        pltpu.VMEM((CHUNK, LANES), F32), pltpu.VMEM((CHUNK, LANES), I32)],
        compiler_params=pltpu.CompilerParams(dimension_semantics=("arbitrary", "arbitrary"),
                                             vmem_limit_bytes=VMEM_LIMIT),
        name="dsa",
    )(qi, wcol, slope, qlat, smallt, c, wuv)


QUERY_TILE = 2 * CHUNK
N_QTILES = SEQ // QUERY_TILE
N_KEY_BLOCKS = SEQ // KEY_BLOCK


def _key_positions(kb):
    return kb * KEY_BLOCK + lax.broadcasted_iota(I32, (KEY_BLOCK, QUERY_TILE), 0)


def _count_keys(sc_ref, nkb, pred):
    def body(kb, acc):
        hit = jnp.where(pred(sc_ref[kb], _key_positions(kb)), 1.0, 0.0)
        return acc + jnp.sum(hit.reshape(KEY_BLOCK // 32, 32, QUERY_TILE), axis=0)
    acc = lax.fori_loop(0, nkb, body, jnp.zeros((32, QUERY_TILE), F32))
    return jnp.sum(acc, axis=0, keepdims=True)


def _dsa_kernel(qidx_ref, wt_ref, qlat_ref, small_ref, c_ref, ct_ref, wuv_ref,
                aout_ref,
                kidx_scr, ctb_scr, sc_ref, thr_ref, jlim_ref, m_ref, l_ref, acc_ref):
    qt = pl.program_id(1)

    @pl.when(qt == 0)
    def _():
        kidx_scr[...] = small_ref[0, :, 0:IDX_DIM].astype(BF16)
        for kb in range(N_KEY_BLOCKS):
            ctb_scr[kb] = ct_ref[:, kb * KEY_BLOCK:(kb + 1) * KEY_BLOCK]

    lane = lax.broadcasted_iota(I32, (1, QUERY_TILE), 1)
    qpos = qt * QUERY_TILE + lane
    key_limit = (qt * 2 + 1 + jnp.where(lane >= CHUNK, 1, 0)) * CHUNK
    nkb = (qt + 2) // 2

    qi2 = [jnp.concatenate([qidx_ref[0, :, h * IDX_DIM:(h + 1) * IDX_DIM] for h in (2 * g, 2 * g + 1)], axis=0)
           for g in range(IDX_HEADS // 2)]
    wt = wt_ref[...]

    def score_body(kb, carry):
        kblk = kidx_scr[pl.ds(pl.multiple_of(kb * KEY_BLOCK, KEY_BLOCK), KEY_BLOCK), :]
        sc = None
        for g in range(IDX_HEADS // 2):
            rel = jnp.maximum(_dot_nt(kblk, qi2[g]), 0.0)
            for k in range(2):
                h = 2 * g + k
                term = rel[:, k * QUERY_TILE:(k + 1) * QUERY_TILE] * wt[h:h + 1, :]
                sc = term if sc is None else sc + term
        sc_ref[kb] = jnp.where(_key_positions(kb) < key_limit, sc, -jnp.inf)
        return carry

    lax.fori_loop(0, nkb, score_body, 0)

    thr_ref[...] = jnp.full(thr_ref.shape, -jnp.inf, F32)
    jlim_ref[...] = jnp.full(jlim_ref.shape, -1, I32)

    def as_score(code):
        return pltpu.bitcast(jnp.where(code < 0, code ^ jnp.int32(0x7FFFFFFF), code), F32)

    @pl.when(qt * QUERY_TILE + CHUNK > K_SEL)
    def _():
        k_sel = jnp.float32(K_SEL)
        cnt0 = _count_keys(sc_ref, nkb, lambda sc, pos: sc >= 0.0)
        t0 = jnp.where(cnt0 >= k_sel, jnp.int32(0), jnp.int32(INT_MIN))

        def bit_body(p, t):
            cand = t + lax.shift_left(jnp.int32(1), jnp.int32(30) - p)
            cand_f = as_score(cand)
            cnt = _count_keys(sc_ref, nkb, lambda sc, pos: sc >= cand_f)
            return jnp.where(cnt >= k_sel, cand, t)

        t = as_score(lax.fori_loop(0, 31, bit_body, t0))
        thr_ref[...] = jnp.broadcast_to(t, thr_ref.shape)
        jlim_ref[...] = jnp.full(jlim_ref.shape, SEQ - 1, I32)
        cnt_ge = _count_keys(sc_ref, nkb, lambda sc, pos: sc >= t)

        @pl.when(jnp.max(cnt_ge) > k_sel)
        def _():
            need = k_sel - _count_keys(sc_ref, nkb, lambda sc, pos: sc > t)

            def pos_body(p, j):
                cand = j + lax.shift_left(jnp.int32(1), jnp.int32(10) - p)
                cnt = _count_keys(sc_ref, nkb, lambda sc, pos: (sc == t) & (pos < cand))
                return jnp.where(cnt < need, cand, j)

            j = lax.fori_loop(0, 11, pos_body, jnp.zeros((1, QUERY_TILE), I32))
            jlim_ref[...] = jnp.broadcast_to(j, jlim_ref.shape)

    thr = thr_ref[0:1, :]
    jlim = jlim_ref[0:1, :]
    qlat_all = jnp.concatenate(
        [qlat_ref[0, :, h * KV_RANK:(h + 1) * KV_RANK] for h in range(A_HEADS)], axis=0)
    m_ref[...] = jnp.full(m_ref.shape, NEG_BIG, F32)
    l_ref[...] = jnp.zeros(l_ref.shape, F32)
    acc_ref[...] = jnp.zeros(acc_ref.shape, F32)

    def att_body(kb, carry):
        cblk = c_ref[0, pl.ds(pl.multiple_of(kb * KEY_BLOCK, KEY_BLOCK), KEY_BLOCK), :]
        s_all = _dot_nt(cblk, qlat_all)
        sc = sc_ref[kb]
        kpos = _key_positions(kb)
        sel = (sc > thr) | ((sc == thr) & (kpos <= jlim))
        bias = jnp.where(sel, 0.0, NEG_BIG)
        negdist = -jnp.abs(qpos - kpos).astype(F32)
        m_all = m_ref[...]
        l_all = l_ref[...]
        ps, alphas, ms, ls = [], [], [], []
        for h in range(A_HEADS):
            s = s_all[:, h * QUERY_TILE:(h + 1) * QUERY_TILE] + (2.0 ** -(h + 1)) * negdist + bias
            m_old = m_all[h:h + 1, :]
            m_new = jnp.maximum(m_old, jnp.max(s, axis=0, keepdims=True))
            alpha = jnp.exp(m_old - m_new)
            p = jnp.exp(s - m_new)
            ls.append(alpha * l_all[h:h + 1, :] + jnp.sum(p, axis=0, keepdims=True))
            ms.append(m_new)
            ps.append(p.astype(BF16))
            alphas.append(alpha)
        m_ref[...] = jnp.concatenate(ms, axis=0)
        l_ref[...] = jnp.concatenate(ls, axis=0)
        pv = _dot(ctb_scr[kb], jnp.concatenate(ps, axis=1))
        acc_ref[...] = acc_ref[...] * jnp.concatenate(alphas, axis=1) + pv
        return carry

    lax.fori_loop(0, nkb, att_body, 0)
    outs = []
    for h in range(A_HEADS):
        sl = slice(h * QUERY_TILE, (h + 1) * QUERY_TILE)
        o_t = (acc_ref[:, sl] / l_ref[h:h + 1, :]).astype(BF16)
        outs.append(lax.dot_general(o_t, wuv_ref[h], (((0,), (0,)), ((), ())), preferred_element_type=F32))
    aout_ref[0] = jnp.concatenate(outs, axis=-1).astype(BF16)


def _dsa_call(qidx, smallt, qlat, small, c, ct, wuv):
    b = qlat.shape[0]
    return pl.pallas_call(
        _dsa_kernel,
        grid=(b, N_QTILES),
        in_specs=[
            pl.BlockSpec((1, QUERY_TILE, IDX_HEADS * IDX_DIM), lambda bi, qt: (bi, qt, 0)),
            pl.BlockSpec((8, QUERY_TILE), lambda bi, qt: (9, bi * N_QTILES + qt)),
            pl.BlockSpec((1, QUERY_TILE, A_HEADS * KV_RANK), lambda bi, qt: (bi, qt, 0)),
            pl.BlockSpec((1, SEQ, LANES), lambda bi, qt: (bi, 0, 0)),
            pl.BlockSpec((1, SEQ, KV_RANK), lambda bi, qt: (bi, 0, 0)),
            pl.BlockSpec((KV_RANK, SEQ), lambda bi, qt: (0, bi)),
            pl.BlockSpec(wuv.shape, lambda bi, qt: (0, 0, 0)),
        ],
        out_specs=pl.BlockSpec((1, QUERY_TILE, A_WIDTH), lambda bi, qt: (bi, qt, 0)),
        out_shape=jax.ShapeDtypeStruct((b, SEQ, A_WIDTH), BF16),
        scratch_shapes=[pltpu.VMEM((SEQ, IDX_DIM), BF16),
                        pltpu.VMEM((N_KEY_BLOCKS, KV_RANK, KEY_BLOCK), BF16),
                        pltpu.VMEM((N_KEY_BLOCKS, KEY_BLOCK, QUERY_TILE), F32),
                        pltpu.VMEM((8, QUERY_TILE), F32), pltpu.VMEM((8, QUERY_TILE), I32),
                        pltpu.VMEM((A_HEADS, QUERY_TILE), F32), pltpu.VMEM((A_HEADS, QUERY_TILE), F32),
                        pltpu.VMEM((KV_RANK, A_HEADS * QUERY_TILE), F32)],
        compiler_params=pltpu.CompilerParams(dimension_semantics=("arbitrary", "arbitrary"),
                                             vmem_limit_bytes=VMEM_LIMIT),
        name="dsa",
    )(qidx, smallt, qlat, small, c, ct, wuv)


def _log_sigmoid(x):
    return jnp.minimum(x, 0.0) - jnp.log(1.0 + jnp.exp(-jnp.abs(x)))


def _mlstm_kernel(qm_ref, km_ref, v_ref, z_ref, gcol_ref, grow_ref, bcol_ref, brow_ref, normg_ref,
                  out_ref,
                  ct_ref, n_ref, m_ref):
    ci = pl.program_id(1)

    @pl.when(ci == 0)
    def _():
        ct_ref[...] = jnp.zeros(ct_ref.shape, F32)
        n_ref[...] = jnp.zeros(n_ref.shape, F32)
        m_ref[...] = jnp.full(m_ref.shape, NEG_BIG, F32)

    hi = lax.Precision.HIGHEST
    gcol = gcol_ref[0] + bcol_ref[...]
    grow = grow_ref[0] + brow_ref[...]
    r_i = lax.broadcasted_iota(I32, (CHUNK, CHUNK), 0)
    c_i = lax.broadcasted_iota(I32, (CHUNK, CHUNK), 1)
    causal = c_i <= r_i
    tril = jnp.where(causal, 1.0, 0.0).astype(F32)
    triu = jnp.where(r_i <= c_i, 1.0, 0.0).astype(F32)
    lf_col = _log_sigmoid(gcol)
    lf_row = _log_sigmoid(grow)
    bcum_col = jnp.dot(tril, lf_col, precision=hi, preferred_element_type=F32)
    bcum_row = jnp.dot(lf_row, triu, precision=hi, preferred_element_type=F32)

    for h in range(M_HEADS):
        sl = slice(h * M_HEAD_DIM, (h + 1) * M_HEAD_DIM)
        q = qm_ref[0, :, sl]
        k = km_ref[0, :, sl]
        v = v_ref[0, :, sl]
        li_col = gcol[:, 72 + h:73 + h]
        li_row = grow[h:h + 1, :]
        b_col = bcum_col[:, 76 + h:77 + h]
        b_row = bcum_row[4 + h:5 + h, :]
        b_tot = b_row[:, CHUNK - 1:CHUNK]
        ct_prev = ct_ref[h]
        n_prev = n_ref[h]
        m_prev = m_ref[h][:, 0:1]

        g = b_col + m_prev
        dlog = jnp.where(causal, b_col - b_row + li_row, NEG_BIG)
        m_t = jnp.maximum(g, jnp.max(dlog, axis=-1, keepdims=True))
        w_intra = jnp.exp(dlog - m_t)
        w_state = jnp.exp(g - m_t)
        s = _dot_nt(q, k) * w_intra
        num = _dot(s.astype(BF16), v) + w_state * _dot(q, ct_prev.astype(BF16))
        den = (jnp.sum(s, axis=-1, keepdims=True)
               + w_state * jnp.sum(q.astype(F32) * n_prev, axis=-1, keepdims=True))
        hc = num / jnp.maximum(jnp.abs(den), jnp.exp(-m_t))
        mu = jnp.mean(hc, axis=-1, keepdims=True)
        var = jnp.mean(jnp.square(hc - mu), axis=-1, keepdims=True)
        hc = (hc - mu) * lax.rsqrt(var + LN_EPS)
        out_ref[0, :, sl] = (jax.nn.sigmoid(z_ref[0, :, sl]) * (hc * normg_ref[:, sl])).astype(BF16)

        a_col = b_tot - b_col + li_col
        a_max = jnp.max(a_col, axis=0, keepdims=True)
        m_new = jnp.maximum(b_tot + m_prev, a_max)
        decay = jnp.exp(b_tot + m_prev - m_new)
        kw = k.astype(F32) * jnp.exp(a_col - m_new)
        ct_ref[h] = decay * ct_prev + _dot(kw.T.astype(BF16), v)
        n_ref[h] = decay * n_prev + jnp.sum(kw, axis=0, keepdims=True)
        m_ref[h] = jnp.broadcast_to(m_new, m_ref.shape[1:])


def _mlstm_call(qm, km, v, z, gcol, grow, bcol, brow, normg):
    b = qm.shape[0]
    tokspec = lambda w: pl.BlockSpec((1, CHUNK, w), lambda bi, ci: (bi, ci, 0))
    full = lambda a: pl.BlockSpec(a.shape, lambda bi, ci: (0,) * a.ndim)
    return pl.pallas_call(
        _mlstm_kernel,
        grid=(b, N_CHUNKS),
        in_specs=[tokspec(M_WIDTH), tokspec(M_WIDTH), tokspec(M_WIDTH), tokspec(M_WIDTH), tokspec(LANES),
                  pl.BlockSpec((1, 8, CHUNK), lambda bi, ci: (bi * N_CHUNKS + ci, 0, 0)),
                  full(bcol), full(brow), full(normg)],
        out_specs=tokspec(M_WIDTH),
        out_shape=jax.ShapeDtypeStruct((b, SEQ, M_WIDTH), BF16),
        scratch_shapes=[pltpu.VMEM((M_HEADS, M_HEAD_DIM, M_HEAD_DIM), F32),
                        pltpu.VMEM((M_HEADS, 1, M_HEAD_DIM), F32),
                        pltpu.VMEM((M_HEADS, 1, LANES), F32)],
        compiler_params=pltpu.CompilerParams(dimension_semantics=("arbitrary", "arbitrary"),
                                             vmem_limit_bytes=VMEM_LIMIT),
        name="mlstm",
    )(qm, km, v, z, gcol, grow, bcol, brow, normg)


def _layer_norm(y, g, b):
    mu = jnp.mean(y, axis=-1, keepdims=True)
    var = jnp.mean(jnp.square(y - mu), axis=-1, keepdims=True)
    return (y - mu) * lax.rsqrt(var + LN_EPS) * g + b


def _to_row_tiles(ref, val):
    n = val.shape[0]
    for k in range(D_MODEL // LANES):
        ref[pl.ds(k, n, stride=8), :] = val[:, k * LANES:(k + 1) * LANES]


def _from_row_tiles(ref, n):
    return jnp.concatenate([ref[pl.ds(k, n, stride=8), :] for k in range(D_MODEL // LANES)], axis=1)


def _mix_kernel(a_ref, m_ref, x_ref, wout_ref, g_ref, b_ref, wr_ref, br_ref,
                h_ref, ht_ref, e_ref, gate_ref, cnt_ref):
    @pl.when(pl.program_id(0) == 0)
    def _():
        cnt_ref[...] = jnp.zeros(cnt_ref.shape, F32)

    mix = _dot(a_ref[...], wout_ref[0:A_WIDTH, :]) + _dot(m_ref[...], wout_ref[A_WIDTH:, :])
    h1 = _layer_norm(DN_ALPHA * x_ref[...] + mix, g_ref[...], b_ref[...])
    h_ref[...] = h1
    _to_row_tiles(ht_ref, h1)
    logits = _dot(h1.astype(BF16), wr_ref[...]) + br_ref[...]
    lane = lax.broadcasted_iota(I32, logits.shape, 1).astype(F32)
    e_slab = jnp.zeros(logits.shape, F32)
    v_slab = jnp.zeros(logits.shape, F32)
    vals = []
    chosen = jnp.zeros(logits.shape, F32)
    for j in range(TOP_EXPERTS):
        mx = jnp.max(logits, axis=-1, keepdims=True)
        idx = jnp.min(jnp.where(logits == mx, lane, float(LANES)), axis=-1, keepdims=True)
        e_slab = jnp.where(lane == float(j), idx, e_slab)
        vals.append(mx)
        chosen = jnp.where(lane == idx, 1.0, chosen)
        logits = jnp.where(lane == idx, -jnp.inf, logits)
    cnt_ref[...] += jnp.sum(chosen, axis=0, keepdims=True)
    ex = [jnp.exp(vj - vals[0]) for vj in vals]
    tot = ex[0] + ex[1] + ex[2] + ex[3]
    for j in range(TOP_EXPERTS):
        v_slab = jnp.where(lane == float(j), ex[j] / tot, v_slab)
    e_ref[...] = e_slab.astype(I32)
    gate_ref[...] = v_slab


def _mix_call(aout, mout, x2, wout, g, b, wr, br):
    t = x2.shape[0]
    rows = PROJ_ROWS
    full = lambda a: pl.BlockSpec(a.shape, lambda i: (0,) * a.ndim)
    tok = lambda w: pl.BlockSpec((rows, w), lambda i: (i, 0))
    return pl.pallas_call(
        _mix_kernel,
        grid=(t // rows,),
        in_specs=[tok(A_WIDTH), tok(M_WIDTH), tok(D_MODEL), full(wout), full(g), full(b), full(wr), full(br)],
        out_specs=(tok(D_MODEL), pl.BlockSpec((8 * rows, LANES), lambda i: (i, 0)), tok(LANES), tok(LANES),
                   pl.BlockSpec((1, LANES), lambda i: (0, 0))),
        out_shape=(jax.ShapeDtypeStruct((t, D_MODEL), F32),
                   jax.ShapeDtypeStruct((8 * t, LANES), F32),
                   jax.ShapeDtypeStruct((t, LANES), I32),
                   jax.ShapeDtypeStruct((t, LANES), F32),
                   jax.ShapeDtypeStruct((1, LANES), F32)),
        compiler_params=pltpu.CompilerParams(dimension_semantics=("arbitrary",),
                                             vmem_limit_bytes=VMEM_LIMIT),
        name="mix",
    )(aout, mout, x2, wout, g, b, wr, br)


ASSIGN_BITS = 18
FFN_SECTIONS = 8
HID_CHUNK = D_MODEL // 4


def _experts_kernel(blk_e_ref, nused_ref,
                    prev_ref, cur_ref, next_ref, h_hbm, wgu_ref, bgu_ref, wd_ref, bd_ref,
                    ys_hbm,
                    xbuf0_ref, xbuf1_ref, hbuf_ref, ybuf0_ref, ybuf1_ref, in_sem, out_sem):
    i = pl.program_id(0)
    nused = nused_ref[0]
    rows = EXPERT_ROWS
    n_tok = h_hbm.shape[0] // 8
    per_sec = rows // FFN_SECTIONS
    xbuf = (xbuf0_ref, xbuf1_ref)
    ybuf = (ybuf0_ref, ybuf1_ref)

    def in_wait(s):
        pltpu.make_async_copy(h_hbm.at[pl.ds(0, 8 * rows)], xbuf[s], in_sem).wait()

    def out_wait(s):
        pltpu.make_async_copy(ybuf[s], ys_hbm.at[pl.ds(0, 8 * rows)], out_sem).wait()

    def tile(r):
        return pl.ds(r * 8, 8) if isinstance(r, int) else pl.ds(pl.multiple_of(r * 8, 8), 8)

    def prio(r):
        return r % 2 if isinstance(r, int) else 0

    def gather_row(a, s, r):
        tok = a & (n_tok - 1)
        pltpu.make_async_copy(h_hbm.at[tile(tok)], xbuf[s].at[tile(r)], in_sem).start(priority=prio(r))

    def scatter_row(a, s, r):
        pltpu.make_async_copy(ybuf[s].at[tile(r)], ys_hbm.at[tile(a)], out_sem).start(priority=1 - prio(r))

    @pl.when(i == 0)
    def _():
        ybuf1_ref[...] = jnp.zeros(ybuf1_ref.shape, F32)

        def first(r, carry):
            gather_row(cur_ref[0, 0, r], 0, r)
            return carry
        lax.fori_loop(0, rows, first, 0)

    def moves(s, lo, hi):
        for r in range(lo, hi):
            scatter_row(prev_ref[0, 0, r], 1 - s, r)
            gather_row(next_ref[0, 0, r], 1 - s, r)

    def body(s):
        in_wait(s)
        xb = _from_row_tiles(xbuf[s], rows).astype(BF16)
        sec = 0
        for c in range(D_MODEL // HID_CHUNK):
            lo, hi = c * HID_CHUNK, (c + 1) * HID_CHUNK
            g = _dot(xb, wgu_ref[0, :, lo:hi]) + bgu_ref[0, :, lo:hi]
            u = _dot(xb, wgu_ref[0, :, D_MODEL + lo:D_MODEL + hi]) + bgu_ref[0, :, D_MODEL + lo:D_MODEL + hi]
            g = jnp.minimum(g, SWIGLU_LIMIT)
            u = jnp.clip(u, -SWIGLU_LIMIT, SWIGLU_LIMIT)
            hbuf_ref[:, lo:hi] = ((u + 1.0) * g * jax.nn.sigmoid(SWIGLU_ALPHA * g)).astype(BF16)
            moves(s, sec * per_sec, (sec + 1) * per_sec)
            sec += 1
        hid = hbuf_ref[...]
        for c in range(D_MODEL // HID_CHUNK):
            lo, hi = c * HID_CHUNK, (c + 1) * HID_CHUNK
            y = _dot(hid, wd_ref[0, :, lo:hi]) + bd_ref[0, :, lo:hi]
            for k in range(HID_CHUNK // LANES):
                ybuf[s][pl.ds(lo // LANES + k, rows, stride=8), :] = y[:, k * LANES:(k + 1) * LANES]
            moves(s, sec * per_sec, (sec + 1) * per_sec)
            sec += 1

    for s in range(2):
        @pl.when((i < nused) & (i % 2 == s))
        def _(s=s):
            @pl.when(i >= 1)
            def _():
                out_wait(s)
            body(s)

        @pl.when((i == nused) & (i % 2 == s))
        def _(s=s):
            in_wait(s)
            out_wait(s)

            def last(r, carry):
                scatter_row(prev_ref[0, 0, r], 1 - s, r)
                return carry
            lax.fori_loop(0, rows, last, 0)
            out_wait(1 - s)


def _experts_call(blk_e, nused, row_dst, h1t, wgu, bgu, wd, bd):
    nb = row_dst.shape[0] - 3
    rows = EXPERT_ROWS
    n_assign = h1t.shape[0] // 8 * TOP_EXPERTS
    wspec = lambda shape: pl.BlockSpec(shape, lambda i, be, nu: (be[i], 0, 0))
    dst_spec = lambda d: pl.BlockSpec((1, 1, rows), lambda i, be, nu: (i + d, 0, 0), memory_space=pltpu.SMEM)
    grid_spec = pltpu.PrefetchScalarGridSpec(
        num_scalar_prefetch=2,
        grid=(nb + 1,),
        in_specs=[
            dst_spec(0), dst_spec(1), dst_spec(2),
            pl.BlockSpec(memory_space=pl.ANY),
            wspec((1, D_MODEL, 2 * D_MODEL)), wspec((1, 1, 2 * D_MODEL)),
            wspec((1, D_MODEL, D_MODEL)), wspec((1, 1, D_MODEL)),
        ],
        out_specs=pl.BlockSpec(memory_space=pl.ANY),
        scratch_shapes=[pltpu.VMEM((8 * rows, LANES), F32), pltpu.VMEM((8 * rows, LANES), F32),
                        pltpu.VMEM((rows, D_MODEL), BF16),
                        pltpu.VMEM((8 * rows, LANES), F32), pltpu.VMEM((8 * rows, LANES), F32),
                        pltpu.SemaphoreType.DMA, pltpu.SemaphoreType.DMA],
    )
    return pl.pallas_call(
        _experts_kernel,
        grid_spec=grid_spec,
        out_shape=jax.ShapeDtypeStruct((8 * (n_assign + rows), LANES), F32),
        compiler_params=pltpu.CompilerParams(dimension_semantics=("arbitrary",),
                                             vmem_limit_bytes=VMEM_LIMIT),
        name="experts",
    )(blk_e, nused, row_dst, row_dst, row_dst, h1t, wgu, bgu, wd, bd)


def _combine_kernel(y0_ref, y1_ref, y2_ref, y3_ref, gate_ref, h_ref, g_ref, b_ref, out_ref):
    n = h_ref.shape[0]
    ffn = gate_ref[:, 0:1] * _from_row_tiles(y0_ref, n)
    for j, y_ref in enumerate((y1_ref, y2_ref, y3_ref), start=1):
        ffn = ffn + gate_ref[:, j:j + 1] * _from_row_tiles(y_ref, n)
    out_ref[...] = _layer_norm(DN_ALPHA * h_ref[...] + ffn, g_ref[...], b_ref[...])


def _combine_call(ys, gates, h1, g, b):
    t = h1.shape[0]
    rows = COMBINE_ROWS
    full = lambda a: pl.BlockSpec(a.shape, lambda i: (0,) * a.ndim)
    choice = lambda j: pl.BlockSpec((8 * rows, LANES), lambda i: (j * (t // rows) + i, 0))
    return pl.pallas_call(
        _combine_kernel,
        grid=(t // rows,),
        in_specs=[
            choice(0), choice(1), choice(2), choice(3),
            pl.BlockSpec((rows, LANES), lambda i: (i, 0)),
            pl.BlockSpec((rows, D_MODEL), lambda i: (i, 0)),
            full(g), full(b),
        ],
        out_specs=pl.BlockSpec((rows, D_MODEL), lambda i: (i, 0)),
        out_shape=jax.ShapeDtypeStruct((t, D_MODEL), F32),
        compiler_params=pltpu.CompilerParams(dimension_semantics=("arbitrary",),
                                             vmem_limit_bytes=VMEM_LIMIT),
        name="combine",
    )(ys, ys, ys, ys, gates, h1, g, b)


def _routing_tables(top_e, counts):
    t = top_e.shape[0]
    a = t * TOP_EXPERTS
    rows = EXPERT_ROWS
    assert t & (t - 1) == 0 and a <= (1 << ASSIGN_BITS)
    ids = jnp.arange(a, dtype=I32)
    keys = jnp.sort((top_e.T.reshape(a) << ASSIGN_BITS) | ids)
    nblk = (counts + rows - 1) // rows
    blk_end = jnp.cumsum(nblk)
    start = jnp.cumsum(counts) - counts
    nb = a // rows + N_EXPERTS
    nused = blk_end[-1].astype(I32)
    b = jnp.arange(nb + 2, dtype=I32)
    used = b < nused
    be = jnp.sum((jnp.minimum(b, nused - 1)[:, None] >= blk_end[None, :]).astype(I32), axis=1)
    onehot = (be[:, None] == jnp.arange(N_EXPERTS, dtype=I32)[None, :]).astype(I32)
    pick = lambda v: jnp.sum(onehot * v[None, :], axis=1)
    within = (b - (pick(blk_end) - pick(nblk))) * rows
    blk_start = jnp.where(used, pick(start) + within, 0).astype(I32)
    blk_n = jnp.where(used, jnp.clip(pick(counts) - within, 0, rows), 0).astype(I32)
    r = jnp.arange(rows, dtype=I32)[None, :]
    window = jnp.minimum(blk_start[:, None] + r, a - 1)
    dump = jnp.broadcast_to(a + r, (nb + 2, rows))
    row_dst = jnp.where(r < blk_n[:, None], keys[window] & ((1 << ASSIGN_BITS) - 1), dump)
    row_dst = jnp.concatenate([dump[:1], row_dst], axis=0).reshape(nb + 3, 1, rows)
    return be.astype(I32), nused.reshape(1), row_dst


def kernel(x, w_in, kv_norm_g, w_uk, w_uv, m_conv_w, m_conv_b, w_mq, w_mk, b_i, b_f, m_norm_g, w_out,
           ln1_g, ln1_b, w_router, b_router, w_gu, b_gu, w_down, b_down, ln2_g, ln2_b):
    bsz, seq, _ = x.shape
    assert seq == SEQ
    t = bsz * seq
    x2 = x.reshape(t, D_MODEL)
    w = w_in[0]
    o_qa, o_ckv, o_qi, o_ki, o_wi, o_u, o_v, o_z, o_i, o_f, o_end = 0, 512, 640, 1152, 1216, 1224, 1736, 2248, 2760, 2764, 2768
    wmain = jnp.concatenate([w[:, o_qa:o_ki], w[:, o_u:o_i]], axis=1).astype(BF16)
    wsmall = jnp.zeros((D_MODEL, LANES), F32).at[:, 0:72].set(w[:, o_ki:o_u]).at[:, 72:80].set(w[:, o_i:o_end]).astype(BF16)
    wsmallt = (jnp.zeros((LANES, D_MODEL), F32).at[0:64].set(w[:, o_ki:o_wi].T).at[64:72].set(w[:, o_i:o_end].T)
               .at[72:80].set(w[:, o_wi:o_u].T).astype(BF16))
    wckvt = w[:, o_ckv:o_qi].T.astype(BF16)
    wukt = jnp.swapaxes(w_uk[0], 1, 2).astype(BF16)

    qlat, c, ct, qidx, small, smallt, qm, km, v, z = _proj_call(
        x2, wmain, wsmall, wsmallt, wckvt, wukt, kv_norm_g[0].reshape(1, KV_RANK),
        kv_norm_g[0].reshape(KV_RANK, 1), m_conv_w[0],
        m_conv_b[0].reshape(1, M_WIDTH), w_mq[0].astype(BF16), w_mk[0].astype(BF16))

    assert A_HEADS == 8
    aout = _dsa_call(qidx.reshape(bsz, seq, IDX_HEADS * IDX_DIM), smallt,
                     qlat.reshape(bsz, seq, A_HEADS * KV_RANK), small.reshape(bsz, seq, LANES),
                     c.reshape(bsz, seq, KV_RANK), ct, w_uv[0].astype(BF16))

    grow = smallt[64:72].reshape(8, bsz * N_CHUNKS, CHUNK).transpose(1, 0, 2)
    bias8 = jnp.concatenate([b_i[0], b_f[0]])
    bcol = jnp.zeros((1, LANES), F32).at[0, 72:80].set(bias8)
    mout = _mlstm_call(qm.reshape(bsz, seq, M_WIDTH), km.reshape(bsz, seq, M_WIDTH),
                       v.reshape(bsz, seq, M_WIDTH), z.reshape(bsz, seq, M_WIDTH),
                       small.reshape(bsz, seq, LANES), grow, bcol, bias8.reshape(8, 1),
                       m_norm_g[0].reshape(1, M_WIDTH))

    wr = jnp.zeros((D_MODEL, LANES), F32).at[:, :N_EXPERTS].set(w_router[0]).astype(BF16)
    br = jnp.full((1, LANES), NEG_BIG, F32).at[0, :N_EXPERTS].set(b_router[0])
    h1, h1t, e_slab, g_slab, cnt = _mix_call(aout.reshape(t, A_WIDTH), mout.reshape(t, M_WIDTH), x2,
                                        w_out[0].astype(BF16), ln1_g[0].reshape(1, D_MODEL),
                                        ln1_b[0].reshape(1, D_MODEL), wr, br)

    blk_e, nused, row_dst = _routing_tables(e_slab[:, :TOP_EXPERTS], cnt[0, :N_EXPERTS].astype(I32))
    ys = _experts_call(blk_e, nused, row_dst, h1t, w_gu[0].astype(BF16),
                       b_gu[0].reshape(N_EXPERTS, 1, 2 * D_MODEL), w_down[0].astype(BF16),
                       b_down[0].reshape(N_EXPERTS, 1, D_MODEL))
    out = _combine_call(ys, g_slab, h1, ln2_g[0].reshape(1, D_MODEL), ln2_b[0].reshape(1, D_MODEL))
    return out.reshape(bsz, seq, D_MODEL)
```

```python
import functools

import jax
import jax.numpy as jnp
import numpy as np
from jax import lax
from jax.experimental import pallas as pl
from jax.experimental.pallas import tpu as pltpu

F32 = jnp.float32
BF16 = jnp.bfloat16
I32 = jnp.int32

D_MODEL = 1024
SEQ = 2048
CHUNK = 64
A_HEADS = 8
A_HEAD_DIM = 64
KV_RANK = 128
IDX_HEADS = 8
IDX_DIM = 64
K_SEL = 256
M_HEADS = 4
M_HEAD_DIM = 128
CONV_WIDTH = 4
N_EXPERTS = 32
TOP_EXPERTS = 4
SWIGLU_LIMIT = 7.0
SWIGLU_ALPHA = 1.702
LN_EPS = 1e-5
DN_ALPHA = 2.0 ** 0.25

A_WIDTH = A_HEADS * A_HEAD_DIM
M_WIDTH = M_HEADS * M_HEAD_DIM
N_CHUNKS = SEQ // CHUNK

LANES = 128
NEG_BIG = -1e30
INT_MIN = -(2 ** 31)

PROJ_ROWS = 512
KEY_BLOCK = 256
EXPERT_ROWS = 256
COMBINE_ROWS = 256
VMEM_LIMIT = 56 * 1024 * 1024

_QA, _CKV, _QI, _U, _V, _Z = 0, 512, 640, 1152, 1664, 2176
MAIN_WIDTH = 2688


def _dot(a, b):
    return jnp.dot(a, b, preferred_element_type=F32)


def _dot_nt(a, b):
    return lax.dot_general(a, b, (((1,), (1,)), ((), ())), preferred_element_type=F32)


def _proj_kernel(x_ref, wmain_ref, wsmall_ref, wsmallt_ref, wckvt_ref, wukt_ref, kvg_ref, kvgcol_ref,
                 convw_ref, convb_ref, wmq_ref, wmk_ref,
                 qlat_ref, c_ref, ct_ref, qidx_ref, small_ref, smallt_ref, qm_ref, km_ref, v_ref, z_ref,
                 uext_ref):
    i = pl.program_id(0)
    rows = x_ref.shape[0]
    xb = x_ref[...].astype(BF16)

    qa = _dot(xb, wmain_ref[:, _QA:_CKV])
    for h in range(A_HEADS):
        qh = qa[:, h * A_HEAD_DIM:(h + 1) * A_HEAD_DIM].astype(BF16)
        ql = _dot(qh, wukt_ref[h]) * (A_HEAD_DIM ** -0.5)
        qlat_ref[:, h * KV_RANK:(h + 1) * KV_RANK] = ql.astype(BF16)

    ckv = _dot(xb, wmain_ref[:, _CKV:_QI])
    ms = jnp.mean(ckv * ckv, axis=-1, keepdims=True)
    c_ref[...] = (ckv * lax.rsqrt(ms + LN_EPS) * kvg_ref[...]).astype(BF16)
    ckv_t = _dot_nt(wckvt_ref[...], xb)
    ms_t = jnp.mean(ckv_t * ckv_t, axis=0, keepdims=True)
    ct_ref[...] = (ckv_t * lax.rsqrt(ms_t + LN_EPS) * kvgcol_ref[...]).astype(BF16)

    qidx_ref[...] = _dot(xb, wmain_ref[:, _QI:_U]).astype(BF16)
    v_ref[...] = _dot(xb, wmain_ref[:, _V:_Z]).astype(BF16)
    z_ref[...] = _dot(xb, wmain_ref[:, _Z:MAIN_WIDTH])
    small_ref[...] = _dot(xb, wsmall_ref[...])
    smallt_ref[...] = _dot_nt(wsmallt_ref[...], xb)

    u = _dot(xb, wmain_ref[:, _U:_V])

    @pl.when(i % (SEQ // rows) == 0)
    def _():
        uext_ref[0:8, :] = jnp.zeros((8, M_WIDTH), F32)

    uext_ref[8:8 + rows, :] = u
    acc = jnp.broadcast_to(convb_ref[...], (rows, M_WIDTH))
    for j in range(CONV_WIDTH):
        off = 8 - (CONV_WIDTH - 1) + j
        acc = acc + convw_ref[j:j + 1, :] * uext_ref[off:off + rows, :]
    uext_ref[0:8, :] = u[rows - 8:rows, :]
    uc = acc * jax.nn.sigmoid(acc)
    for h in range(M_HEADS):
        sl = slice(h * M_HEAD_DIM, (h + 1) * M_HEAD_DIM)
        uh = uc[:, sl].astype(BF16)
        qm_ref[:, sl] = _dot(uh, wmq_ref[h]).astype(BF16)
        km_ref[:, sl] = (_dot(uh, wmk_ref[h]) * (M_HEAD_DIM ** -0.5)).astype(BF16)


def _proj_call(x2, wmain, wsmall, wsmallt, wckvt, wukt, kvg, kvgcol, convw, convb, wmq, wmk):
    t = x2.shape[0]
    rows = PROJ_ROWS
    full = lambda a: pl.BlockSpec(a.shape, lambda i: (0,) * a.ndim)
    tok = lambda w: pl.BlockSpec((rows, w), lambda i: (i, 0))
    tok_t = pl.BlockSpec((LANES, rows), lambda i: (0, i))
    out_shape = (
        jax.ShapeDtypeStruct((t, A_HEADS * KV_RANK), BF16),
        jax.ShapeDtypeStruct((t, KV_RANK), BF16),
        jax.ShapeDtypeStruct((KV_RANK, t), BF16),
        jax.ShapeDtypeStruct((t, IDX_HEADS * IDX_DIM), BF16),
        jax.ShapeDtypeStruct((t, LANES), F32),
        jax.ShapeDtypeStruct((LANES, t), F32),
        jax.ShapeDtypeStruct((t, M_WIDTH), BF16),
        jax.ShapeDtypeStruct((t, M_WIDTH), BF16),
        jax.ShapeDtypeStruct((t, M_WIDTH), BF16),
        jax.ShapeDtypeStruct((t, M_WIDTH), F32),
    )
    out_specs = (tok(A_HEADS * KV_RANK), tok(KV_RANK), tok_t, tok(IDX_HEADS * IDX_DIM), tok(LANES), tok_t,
                 tok(M_WIDTH), tok(M_WIDTH), tok(M_WIDTH), tok(M_WIDTH))
    return pl.pallas_call(
        _proj_kernel,
        grid=(t // rows,),
        in_specs=[tok(D_MODEL), full(wmain), full(wsmall), full(wsmallt), full(wckvt), full(wukt), full(kvg),
                  full(kvgcol), full(convw), full(convb), full(wmq), full(wmk)],
        out_specs=out_specs,
        out_shape=out_shape,
        scratch_shapes=[pltpu.VMEM((8 + rows, M_WIDTH), F32)],
        compiler_params=pltpu.CompilerParams(dimension_semantics=("arbitrary",),
                                             vmem_limit_bytes=VMEM_LIMIT),
        name="proj",
    )(x2, wmain, wsmall, wsmallt, wckvt, wukt, kvg, kvgcol, convw, convb, wmq, wmk)


def _count_rows(keys_ref, nkb, pred):
    def body(kb, acc):
        key = keys_ref[kb]
        pos = kb * KEY_BLOCK + lax.broadcasted_iota(I32, (CHUNK, KEY_BLOCK), 1)
        return acc + jnp.where(pred(key, pos), 1.0, 0.0)
    acc = lax.fori_loop(0, nkb, body, jnp.zeros((CHUNK, KEY_BLOCK), F32))
    return jnp.sum(acc, axis=-1, keepdims=True)


def _dsa_kernel(qi_ref, wcol_ref, slope_ref, qlat_ref, kidxt_ref, c_ref, wuv_ref,
                aout_ref,
                kt_ref, keys_ref, thr_ref, jlim_ref):
    ci = pl.program_id(1)
    hrows = A_HEADS * CHUNK

    @pl.when(ci == 0)
    def _():
        for kb in range(SEQ // KEY_BLOCK):
            kt_ref[kb] = kidxt_ref[:, kb * KEY_BLOCK:(kb + 1) * KEY_BLOCK].astype(BF16)

    nkb = ci // (KEY_BLOCK // CHUNK) + 1
    nkeys = (ci + 1) * CHUNK
    qi = qi_ref[0, 0]
    wcol = wcol_ref[0, 0]

    def score_body(kb, carry):
        rel = jnp.maximum(_dot(qi, kt_ref[kb]), 0.0) * wcol
        sc = rel[0:CHUNK]
        for h in range(1, IDX_HEADS):
            sc = sc + rel[h * CHUNK:(h + 1) * CHUNK]
        pos = kb * KEY_BLOCK + lax.broadcasted_iota(I32, (CHUNK, KEY_BLOCK), 1)
        keys_ref[kb] = jnp.where(pos < nkeys, sc, -jnp.inf)
        return carry

    lax.fori_loop(0, nkb, score_body, 0)

    thr_ref[...] = jnp.full(thr_ref.shape, -jnp.inf, F32)
    jlim_ref[...] = jnp.full(jlim_ref.shape, -1, I32)

    def as_score(code):
        return pltpu.bitcast(jnp.where(code < 0, code ^ jnp.int32(0x7FFFFFFF), code), F32)

    @pl.when(nkeys > K_SEL)
    def _():
        k_sel = jnp.float32(K_SEL)
        cnt0 = _count_rows(keys_ref, nkb, lambda key, pos: key >= 0.0)
        t0 = jnp.where(cnt0 >= k_sel, jnp.int32(0), jnp.int32(INT_MIN))

        def bit_body(p, t):
            cand = t + lax.shift_left(jnp.int32(1), jnp.int32(30) - p)
            cand_f = as_score(cand)
            cnt = _count_rows(keys_ref, nkb, lambda key, pos: key >= cand_f)
            return jnp.where(cnt >= k_sel, cand, t)

        t = as_score(lax.fori_loop(0, 31, bit_body, t0))
        thr_ref[...] = jnp.broadcast_to(t, thr_ref.shape)
        jlim_ref[...] = jnp.full(jlim_ref.shape, SEQ - 1, I32)
        cnt_ge = _count_rows(keys_ref, nkb, lambda key, pos: key >= t)

        @pl.when(jnp.max(cnt_ge) > k_sel)
        def _():
            cnt_gt = _count_rows(keys_ref, nkb, lambda key, pos: key > t)
            need = k_sel - cnt_gt

            def pos_body(p, j):
                cand = j + lax.shift_left(jnp.int32(1), jnp.int32(10) - p)
                cnt = _count_rows(keys_ref, nkb, lambda key, pos: (key == t) & (pos < cand))
                return jnp.where(cnt < need, cand, j)

            j = lax.fori_loop(0, 11, pos_body, jnp.zeros((CHUNK, 1), I32))
            jlim_ref[...] = jnp.broadcast_to(j, jlim_ref.shape)

    thr = thr_ref[:, 0:1]
    jlim = jlim_ref[:, 0:1]
    qlat = jnp.concatenate(
        [qlat_ref[0, :, h * KV_RANK:(h + 1) * KV_RANK] for h in range(A_HEADS)], axis=0)
    slope = slope_ref[...].reshape(A_HEADS, CHUNK, 1)
    qpos = ci * CHUNK + lax.broadcasted_iota(I32, (CHUNK, KEY_BLOCK), 0)

    def att_body(kb, carry):
        m, l, acc = carry
        off = pl.multiple_of(kb * KEY_BLOCK, KEY_BLOCK)
        cblk = c_ref[0, pl.ds(off, KEY_BLOCK), :]
        s = _dot_nt(qlat, cblk)
        key = keys_ref[kb]
        pos = off + lax.broadcasted_iota(I32, (CHUNK, KEY_BLOCK), 1)
        sel = (key > thr) | ((key == thr) & (pos <= jlim))
        bias = jnp.where(sel, 0.0, NEG_BIG)
        dist = jnp.abs(qpos - pos).astype(F32)
        s = (s.reshape(A_HEADS, CHUNK, KEY_BLOCK) - slope * dist[None] + bias[None]).reshape(hrows, KEY_BLOCK)
        m_new = jnp.maximum(m, jnp.max(s, axis=-1, keepdims=True))
        alpha = jnp.exp(m - m_new)
        p = jnp.exp(s - m_new)
        l_new = alpha * l + jnp.sum(p, axis=-1, keepdims=True)
        acc_new = alpha * acc + _dot(p.astype(BF16), cblk)
        return m_new, l_new, acc_new

    init = (jnp.full((hrows, 1), NEG_BIG, F32), jnp.zeros((hrows, 1), F32),
            jnp.zeros((hrows, KV_RANK), F32))
    _, l, acc = lax.fori_loop(0, nkb, att_body, init)
    o = (acc / l).astype(BF16)
    outs = [_dot(o[h * CHUNK:(h + 1) * CHUNK], wuv_ref[h]) for h in range(A_HEADS)]
    aout_ref[0] = jnp.concatenate(outs, axis=-1).astype(BF16)


def _dsa_call(qi, wcol, slope, qlat, smallt, c, wuv):
    b = qlat.shape[0]
    return pl.pallas_call(
        _dsa_kernel,
        grid=(b, N_CHUNKS),
        in_specs=[
            pl.BlockSpec((1, 1, A_HEADS * CHUNK, IDX_DIM), lambda bi, ci: (bi, ci, 0, 0)),
            pl.BlockSpec((1, 1, A_HEADS * CHUNK, 1), lambda bi, ci: (bi, ci, 0, 0)),
            pl.BlockSpec((A_HEADS * CHUNK, 1), lambda bi, ci: (0, 0)),
            pl.BlockSpec((1, CHUNK, A_HEADS * KV_RANK), lambda bi, ci: (bi, ci, 0)),
            pl.BlockSpec((IDX_DIM, SEQ), lambda bi, ci: (0, bi)),
            pl.BlockSpec((1, SEQ, KV_RANK), lambda bi, ci: (bi, 0, 0)),
            pl.BlockSpec(wuv.shape, lambda bi, ci: (0, 0, 0)),
        ],
        out_specs=pl.BlockSpec((1, CHUNK, A_WIDTH), lambda bi, ci: (bi, ci, 0)),
        out_shape=jax.ShapeDtypeStruct((b, SEQ, A_WIDTH), BF16),
        scratch_shapes=[pltpu.VMEM((SEQ // KEY_BLOCK, IDX_DIM, KEY_BLOCK), BF16),
                        pltpu.VMEM((SEQ // KEY_BLOCK, CHUNK, KEY_BLOCK), F32),
                        pltpu.VMEM((CHUNK, LANES), F32), pltpu.VMEM((CHUNK, LANES), I32)],
        compiler_params=pltpu.CompilerParams(dimension_semantics=("arbitrary", "arbitrary"),
                                             vmem_limit_bytes=VMEM_LIMIT),
        name="dsa",
    )(qi, wcol, slope, qlat, smallt, c, wuv)


QUERY_TILE = 2 * CHUNK
N_QTILES = SEQ // QUERY_TILE
N_KEY_BLOCKS = SEQ // KEY_BLOCK


def _key_positions(kb):
    return kb * KEY_BLOCK + lax.broadcasted_iota(I32, (KEY_BLOCK, QUERY_TILE), 0)


def _count_keys(sc_ref, nkb, pred):
    def body(kb, acc):
        hit = jnp.where(pred(sc_ref[kb], _key_positions(kb)), 1.0, 0.0)
        return acc + jnp.sum(hit.reshape(KEY_BLOCK // 32, 32, QUERY_TILE), axis=0)
    acc = lax.fori_loop(0, nkb, body, jnp.zeros((32, QUERY_TILE), F32))
    return jnp.sum(acc, axis=0, keepdims=True)


def _dsa_kernel(qidx_ref, wt_ref, qlat_ref, small_ref, c_ref, ct_ref, wuv_ref,
                aout_ref,
                kidx_scr, ctb_scr, sc_ref, thr_ref, jlim_ref, m_ref, l_ref, acc_ref):
    qt = pl.program_id(1)

    @pl.when(qt == 0)
    def _():
        kidx_scr[...] = small_ref[0, :, 0:IDX_DIM].astype(BF16)
        for kb in range(N_KEY_BLOCKS):
            ctb_scr[kb] = ct_ref[:, kb * KEY_BLOCK:(kb + 1) * KEY_BLOCK]

    lane = lax.broadcasted_iota(I32, (1, QUERY_TILE), 1)
    qpos = qt * QUERY_TILE + lane
    key_limit = (qt * 2 + 1 + jnp.where(lane >= CHUNK, 1, 0)) * CHUNK
    nkb = (qt + 2) // 2

    qi2 = [jnp.concatenate([qidx_ref[0, :, h * IDX_DIM:(h + 1) * IDX_DIM] for h in (2 * g, 2 * g + 1)], axis=0)
           for g in range(IDX_HEADS // 2)]
    wt = wt_ref[...]

    def score_body(kb, carry):
        kblk = kidx_scr[pl.ds(pl.multiple_of(kb * KEY_BLOCK, KEY_BLOCK), KEY_BLOCK), :]
        sc = None
        for g in range(IDX_HEADS // 2):
            rel = jnp.maximum(_dot_nt(kblk, qi2[g]), 0.0)
            for k in range(2):
                h = 2 * g + k
                term = rel[:, k * QUERY_TILE:(k + 1) * QUERY_TILE] * wt[h:h + 1, :]
                sc = term if sc is None else sc + term
        sc_ref[kb] = jnp.where(_key_positions(kb) < key_limit, sc, -jnp.inf)
        return carry

    lax.fori_loop(0, nkb, score_body, 0)

    thr_ref[...] = jnp.full(thr_ref.shape, -jnp.inf, F32)
    jlim_ref[...] = jnp.full(jlim_ref.shape, -1, I32)

    def as_score(code):
        return pltpu.bitcast(jnp.where(code < 0, code ^ jnp.int32(0x7FFFFFFF), code), F32)

    @pl.when(qt * QUERY_TILE + CHUNK > K_SEL)
    def _():
        k_sel = jnp.float32(K_SEL)
        cnt0 = _count_keys(sc_ref, nkb, lambda sc, pos: sc >= 0.0)
        t0 = jnp.where(cnt0 >= k_sel, jnp.int32(0), jnp.int32(INT_MIN))

        def bit_body(p, t):
            cand = t + lax.shift_left(jnp.int32(1), jnp.int32(30) - p)
            cand_f = as_score(cand)
            cnt = _count_keys(sc_ref, nkb, lambda sc, pos: sc >= cand_f)
            return jnp.where(cnt >= k_sel, cand, t)

        t = as_score(lax.fori_loop(0, 31, bit_body, t0))
        thr_ref[...] = jnp.broadcast_to(t, thr_ref.shape)
        jlim_ref[...] = jnp.full(jlim_ref.shape, SEQ - 1, I32)
        cnt_ge = _count_keys(sc_ref, nkb, lambda sc, pos: sc >= t)

        @pl.when(jnp.max(cnt_ge) > k_sel)
        def _():
            need = k_sel - _count_keys(sc_ref, nkb, lambda sc, pos: sc > t)

            def pos_body(p, j):
                cand = j + lax.shift_left(jnp.int32(1), jnp.int32(10) - p)
                cnt = _count_keys(sc_ref, nkb, lambda sc, pos: (sc == t) & (pos < cand))
                return jnp.where(cnt < need, cand, j)

            j = lax.fori_loop(0, 11, pos_body, jnp.zeros((1, QUERY_TILE), I32))
            jlim_ref[...] = jnp.broadcast_to(j, jlim_ref.shape)

    thr = thr_ref[0:1, :]
    jlim = jlim_ref[0:1, :]
    qlat_all = jnp.concatenate(
        [qlat_ref[0, :, h * KV_RANK:(h + 1) * KV_RANK] for h in range(A_HEADS)], axis=0)
    m_ref[...] = jnp.full(m_ref.shape, NEG_BIG, F32)
    l_ref[...] = jnp.zeros(l_ref.shape, F32)
    acc_ref[...] = jnp.zeros(acc_ref.shape, F32)

    def att_body(kb, carry):
        cblk = c_ref[0, pl.ds(pl.multiple_of(kb * KEY_BLOCK, KEY_BLOCK), KEY_BLOCK), :]
        s_all = _dot_nt(cblk, qlat_all)
        sc = sc_ref[kb]
        kpos = _key_positions(kb)
        sel = (sc > thr) | ((sc == thr) & (kpos <= jlim))
        bias = jnp.where(sel, 0.0, NEG_BIG)
        negdist = -jnp.abs(qpos - kpos).astype(F32)
        m_all = m_ref[...]
        l_all = l_ref[...]
        ps, alphas, ms, ls = [], [], [], []
        for h in range(A_HEADS):
            s = s_all[:, h * QUERY_TILE:(h + 1) * QUERY_TILE] + (2.0 ** -(h + 1)) * negdist + bias
            m_old = m_all[h:h + 1, :]
            m_new = jnp.maximum(m_old, jnp.max(s, axis=0, keepdims=True))
            alpha = jnp.exp(m_old - m_new)
            p = jnp.exp(s - m_new)
            ls.append(alpha * l_all[h:h + 1, :] + jnp.sum(p, axis=0, keepdims=True))
            ms.append(m_new)
            ps.append(p.astype(BF16))
            alphas.append(alpha)
        m_ref[...] = jnp.concatenate(ms, axis=0)
        l_ref[...] = jnp.concatenate(ls, axis=0)
        pv = _dot(ctb_scr[kb], jnp.concatenate(ps, axis=1))
        acc_ref[...] = acc_ref[...] * jnp.concatenate(alphas, axis=1) + pv
        return carry

    lax.fori_loop(0, nkb, att_body, 0)
    outs = []
    for h in range(A_HEADS):
        sl = slice(h * QUERY_TILE, (h + 1) * QUERY_TILE)
        o_t = (acc_ref[:, sl] / l_ref[h:h + 1, :]).astype(BF16)
        outs.append(lax.dot_general(o_t, wuv_ref[h], (((0,), (0,)), ((), ())), preferred_element_type=F32))
    aout_ref[0] = jnp.concatenate(outs, axis=-1).astype(BF16)


def _dsa_call(qidx, smallt, qlat, small, c, ct, wuv):
    b = qlat.shape[0]
    return pl.pallas_call(
        _dsa_kernel,
        grid=(b, N_QTILES),
        in_specs=[
            pl.BlockSpec((1, QUERY_TILE, IDX_HEADS * IDX_DIM), lambda bi, qt: (bi, qt, 0)),
            pl.BlockSpec((8, QUERY_TILE), lambda bi, qt: (9, bi * N_QTILES + qt)),
            pl.BlockSpec((1, QUERY_TILE, A_HEADS * KV_RANK), lambda bi, qt: (bi, qt, 0)),
            pl.BlockSpec((1, SEQ, LANES), lambda bi, qt: (bi, 0, 0)),
            pl.BlockSpec((1, SEQ, KV_RANK), lambda bi, qt: (bi, 0, 0)),
            pl.BlockSpec((KV_RANK, SEQ), lambda bi, qt: (0, bi)),
            pl.BlockSpec(wuv.shape, lambda bi, qt: (0, 0, 0)),
        ],
        out_specs=pl.BlockSpec((1, QUERY_TILE, A_WIDTH), lambda bi, qt: (bi, qt, 0)),
        out_shape=jax.ShapeDtypeStruct((b, SEQ, A_WIDTH), BF16),
        scratch_shapes=[pltpu.VMEM((SEQ, IDX_DIM), BF16),
                        pltpu.VMEM((N_KEY_BLOCKS, KV_RANK, KEY_BLOCK), BF16),
                        pltpu.VMEM((N_KEY_BLOCKS, KEY_BLOCK, QUERY_TILE), F32),
                        pltpu.VMEM((8, QUERY_TILE), F32), pltpu.VMEM((8, QUERY_TILE), I32),
                        pltpu.VMEM((A_HEADS, QUERY_TILE), F32), pltpu.VMEM((A_HEADS, QUERY_TILE), F32),
                        pltpu.VMEM((KV_RANK, A_HEADS * QUERY_TILE), F32)],
        compiler_params=pltpu.CompilerParams(dimension_semantics=("arbitrary", "arbitrary"),
                                             vmem_limit_bytes=VMEM_LIMIT),
        name="dsa",
    )(qidx, smallt, qlat, small, c, ct, wuv)


def _log_sigmoid(x):
    return jnp.minimum(x, 0.0) - jnp.log(1.0 + jnp.exp(-jnp.abs(x)))


def _mlstm_kernel(qm_ref, km_ref, v_ref, z_ref, gcol_ref, grow_ref, bcol_ref, brow_ref, normg_ref,
                  out_ref,
                  ct_ref, n_ref, m_ref):
    ci = pl.program_id(1)

    @pl.when(ci == 0)
    def _():
        ct_ref[...] = jnp.zeros(ct_ref.shape, F32)
        n_ref[...] = jnp.zeros(n_ref.shape, F32)
        m_ref[...] = jnp.full(m_ref.shape, NEG_BIG, F32)

    hi = lax.Precision.HIGHEST
    r_i = lax.broadcasted_iota(I32, (CHUNK, CHUNK), 0)
    c_i = lax.broadcasted_iota(I32, (CHUNK, CHUNK), 1)
    causal = c_i <= r_i
    tril = jnp.where(causal, 1.0, 0.0).astype(F32)
    triu = jnp.where(r_i <= c_i, 1.0, 0.0).astype(F32)

    for bb, h in [(bb, h) for bb in range(qm_ref.shape[0]) for h in range(M_HEADS)]:
        if h == 0:
            gcol = gcol_ref[bb] + bcol_ref[...]
            grow = grow_ref[bb, 0] + brow_ref[...]
            bcum_col = jnp.dot(tril, _log_sigmoid(gcol), precision=hi, preferred_element_type=F32)
            bcum_row = jnp.dot(_log_sigmoid(grow), triu, precision=hi, preferred_element_type=F32)
        st = bb * M_HEADS + h
        sl = slice(h * M_HEAD_DIM, (h + 1) * M_HEAD_DIM)
        q = qm_ref[bb, :, sl]
        k = km_ref[bb, :, sl]
        v = v_ref[bb, :, sl]
        li_col = gcol[:, 72 + h:73 + h]
        li_row = grow[h:h + 1, :]
        b_col = bcum_col[:, 76 + h:77 + h]
        b_row = bcum_row[4 + h:5 + h, :]
        b_tot = b_row[:, CHUNK - 1:CHUNK]
        ct_prev = ct_ref[st]
        n_prev = n_ref[st]
        m_prev = m_ref[st][:, 0:1]

        g = b_col + m_prev
        dlog = jnp.where(causal, b_col - b_row + li_row, NEG_BIG)
        m_t = jnp.maximum(g, jnp.max(dlog, axis=-1, keepdims=True))
        w_intra = jnp.exp(dlog - m_t)
        w_state = jnp.exp(g - m_t)
        s = _dot_nt(q, k) * w_intra
        num = _dot(s.astype(BF16), v) + w_state * _dot(q, ct_prev.astype(BF16))
        den = (jnp.sum(s, axis=-1, keepdims=True)
               + w_state * jnp.sum(q.astype(F32) * n_prev, axis=-1, keepdims=True))
        hc = num / jnp.maximum(jnp.abs(den), jnp.exp(-m_t))
        mu = jnp.mean(hc, axis=-1, keepdims=True)
        var = jnp.mean(jnp.square(hc - mu), axis=-1, keepdims=True)
        hc = (hc - mu) * lax.rsqrt(var + LN_EPS)
        out_ref[bb, :, sl] = (jax.nn.sigmoid(z_ref[bb, :, sl]) * (hc * normg_ref[:, sl])).astype(BF16)

        a_col = b_tot - b_col + li_col
        a_max = jnp.max(a_col, axis=0, keepdims=True)
        m_new = jnp.maximum(b_tot + m_prev, a_max)
        decay = jnp.exp(b_tot + m_prev - m_new)
        kw = k.astype(F32) * jnp.exp(a_col - m_new)
        ct_ref[st] = decay * ct_prev + _dot(kw.T.astype(BF16), v)
        n_ref[st] = decay * n_prev + jnp.sum(kw, axis=0, keepdims=True)
        m_ref[st] = jnp.broadcast_to(m_new, m_ref.shape[1:])


MLSTM_GROUP = 2


def _mlstm_call(qm, km, v, z, gcol, grow, bcol, brow, normg):
    b = qm.shape[0]
    grp = MLSTM_GROUP if b % MLSTM_GROUP == 0 else 1
    tokspec = lambda w: pl.BlockSpec((grp, CHUNK, w), lambda bi, ci: (bi, ci, 0))
    full = lambda a: pl.BlockSpec(a.shape, lambda bi, ci: (0,) * a.ndim)
    return pl.pallas_call(
        _mlstm_kernel,
        grid=(b // grp, N_CHUNKS),
        in_specs=[tokspec(M_WIDTH), tokspec(M_WIDTH), tokspec(M_WIDTH), tokspec(M_WIDTH), tokspec(LANES),
                  pl.BlockSpec((grp, 1, 8, CHUNK), lambda bi, ci: (bi, ci, 0, 0)),
                  full(bcol), full(brow), full(normg)],
        out_specs=tokspec(M_WIDTH),
        out_shape=jax.ShapeDtypeStruct((b, SEQ, M_WIDTH), BF16),
        scratch_shapes=[pltpu.VMEM((grp * M_HEADS, M_HEAD_DIM, M_HEAD_DIM), F32),
                        pltpu.VMEM((grp * M_HEADS, 1, M_HEAD_DIM), F32),
                        pltpu.VMEM((grp * M_HEADS, 1, LANES), F32)],
        compiler_params=pltpu.CompilerParams(dimension_semantics=("arbitrary", "arbitrary"),
                                             vmem_limit_bytes=VMEM_LIMIT),
        name="mlstm",
    )(qm, km, v, z, gcol, grow, bcol, brow, normg)


def _layer_norm(y, g, b):
    mu = jnp.mean(y, axis=-1, keepdims=True)
    var = jnp.mean(jnp.square(y - mu), axis=-1, keepdims=True)
    return (y - mu) * lax.rsqrt(var + LN_EPS) * g + b


def _to_row_tiles(ref, val):
    n = val.shape[0]
    for k in range(D_MODEL // LANES):
        ref[pl.ds(k, n, stride=8), :] = val[:, k * LANES:(k + 1) * LANES]


def _from_row_tiles(ref, n):
    return jnp.concatenate([ref[pl.ds(k, n, stride=8), :] for k in range(D_MODEL // LANES)], axis=1)


def _mix_kernel(a_ref, m_ref, x_ref, wout_ref, g_ref, b_ref, wr_ref, br_ref,
                h_ref, ht_ref, e_ref, gate_ref, cnt_ref):
    @pl.when(pl.program_id(0) == 0)
    def _():
        cnt_ref[...] = jnp.zeros(cnt_ref.shape, F32)

    mix = _dot(a_ref[...], wout_ref[0:A_WIDTH, :]) + _dot(m_ref[...], wout_ref[A_WIDTH:, :])
    h1 = _layer_norm(DN_ALPHA * x_ref[...] + mix, g_ref[...], b_ref[...])
    h_ref[...] = h1
    _to_row_tiles(ht_ref, h1)
    logits = _dot(h1.astype(BF16), wr_ref[...]) + br_ref[...]
    lane = lax.broadcasted_iota(I32, logits.shape, 1).astype(F32)
    e_slab = jnp.zeros(logits.shape, F32)
    v_slab = jnp.zeros(logits.shape, F32)
    vals = []
    chosen = jnp.zeros(logits.shape, F32)
    for j in range(TOP_EXPERTS):
        mx = jnp.max(logits, axis=-1, keepdims=True)
        idx = jnp.min(jnp.where(logits == mx, lane, float(LANES)), axis=-1, keepdims=True)
        e_slab = jnp.where(lane == float(j), idx, e_slab)
        vals.append(mx)
        chosen = jnp.where(lane == idx, 1.0, chosen)
        logits = jnp.where(lane == idx, -jnp.inf, logits)
    cnt_ref[...] += jnp.sum(chosen, axis=0, keepdims=True)
    ex = [jnp.exp(vj - vals[0]) for vj in vals]
    tot = ex[0] + ex[1] + ex[2] + ex[3]
    for j in range(TOP_EXPERTS):
        v_slab = jnp.where(lane == float(j), ex[j] / tot, v_slab)
    e_ref[...] = e_slab.astype(I32)
    gate_ref[...] = v_slab


def _mix_call(aout, mout, x2, wout, g, b, wr, br):
    t = x2.shape[0]
    rows = PROJ_ROWS
    full = lambda a: pl.BlockSpec(a.shape, lambda i: (0,) * a.ndim)
    tok = lambda w: pl.BlockSpec((rows, w), lambda i: (i, 0))
    return pl.pallas_call(
        _mix_kernel,
        grid=(t // rows,),
        in_specs=[tok(A_WIDTH), tok(M_WIDTH), tok(D_MODEL), full(wout), full(g), full(b), full(wr), full(br)],
        out_specs=(tok(D_MODEL), pl.BlockSpec((8 * rows, LANES), lambda i: (i, 0)), tok(LANES), tok(LANES),
                   pl.BlockSpec((1, LANES), lambda i: (0, 0))),
        out_shape=(jax.ShapeDtypeStruct((t, D_MODEL), F32),
                   jax.ShapeDtypeStruct((8 * t, LANES), F32),
                   jax.ShapeDtypeStruct((t, LANES), I32),
                   jax.ShapeDtypeStruct((t, LANES), F32),
                   jax.ShapeDtypeStruct((1, LANES), F32)),
        compiler_params=pltpu.CompilerParams(dimension_semantics=("arbitrary",),
                                             vmem_limit_bytes=VMEM_LIMIT),
        name="mix",
    )(aout, mout, x2, wout, g, b, wr, br)


ASSIGN_BITS = 18
FFN_SECTIONS = 8
HID_CHUNK = D_MODEL // 4
RING = 3


def _experts_kernel(blk_e_ref, nused_ref,
                    prev_ref, cur_ref, next_ref, next2_ref, h_hbm, wgu_ref, bgu_ref, wd_ref, bd_ref,
                    ys_hbm,
                    xbuf0_ref, xbuf1_ref, xbuf2_ref, hbuf_ref, ybuf0_ref, ybuf1_ref, ybuf2_ref, in_sem, out_sem):
    i = pl.program_id(0)
    nused = nused_ref[0]
    rows = EXPERT_ROWS
    n_tok = h_hbm.shape[0] // 8
    per_sec = rows // FFN_SECTIONS
    xbuf = (xbuf0_ref, xbuf1_ref, xbuf2_ref)
    ybuf = (ybuf0_ref, ybuf1_ref, ybuf2_ref)

    def in_wait(s):
        pltpu.make_async_copy(h_hbm.at[pl.ds(0, 8 * rows)], xbuf[s], in_sem.at[s]).wait()

    def out_wait(s):
        pltpu.make_async_copy(ybuf[s], ys_hbm.at[pl.ds(0, 8 * rows)], out_sem.at[s]).wait()

    def tile(r):
        return pl.ds(r * 8, 8) if isinstance(r, int) else pl.ds(pl.multiple_of(r * 8, 8), 8)

    def prio(r):
        return r % 2 if isinstance(r, int) else 0

    def gather_row(a, s, r):
        tok = a & (n_tok - 1)
        pltpu.make_async_copy(h_hbm.at[tile(tok)], xbuf[s].at[tile(r)], in_sem.at[s]).start(priority=prio(r))

    def scatter_row(a, s, r):
        pltpu.make_async_copy(ybuf[s].at[tile(r)], ys_hbm.at[tile(a)], out_sem.at[s]).start(priority=1 - prio(r))

    @pl.when(i == 0)
    def _():
        ybuf2_ref[...] = jnp.zeros(ybuf2_ref.shape, F32)

        def first(r, carry):
            gather_row(cur_ref[0, 0, r], 0, r)
            gather_row(next_ref[0, 0, r], 1, r)
            return carry
        lax.fori_loop(0, rows, first, 0)

    def moves(s, lo, hi):
        t = (s + 2) % RING
        for r in range(lo, hi):
            scatter_row(prev_ref[0, 0, r], t, r)
            gather_row(next2_ref[0, 0, r], t, r)

    def body(s):
        in_wait(s)
        xb = _from_row_tiles(xbuf[s], rows).astype(BF16)
        sec = 0
        for c in range(D_MODEL // HID_CHUNK):
            lo, hi = c * HID_CHUNK, (c + 1) * HID_CHUNK
            g = _dot(xb, wgu_ref[0, :, lo:hi]) + bgu_ref[0, :, lo:hi]
            u = _dot(xb, wgu_ref[0, :, D_MODEL + lo:D_MODEL + hi]) + bgu_ref[0, :, D_MODEL + lo:D_MODEL + hi]
            g = jnp.minimum(g, SWIGLU_LIMIT)
            u = jnp.clip(u, -SWIGLU_LIMIT, SWIGLU_LIMIT)
            hbuf_ref[:, lo:hi] = ((u + 1.0) * g * jax.nn.sigmoid(SWIGLU_ALPHA * g)).astype(BF16)
            moves(s, sec * per_sec, (sec + 1) * per_sec)
            sec += 1
        hid = hbuf_ref[...]
        for c in range(D_MODEL // HID_CHUNK):
            lo, hi = c * HID_CHUNK, (c + 1) * HID_CHUNK
            y = _dot(hid, wd_ref[0, :, lo:hi]) + bd_ref[0, :, lo:hi]
            for k in range(HID_CHUNK // LANES):
                ybuf[s][pl.ds(lo // LANES + k, rows, stride=8), :] = y[:, k * LANES:(k + 1) * LANES]
            moves(s, sec * per_sec, (sec + 1) * per_sec)
            sec += 1

    for s in range(RING):
        @pl.when((i < nused) & (i % RING == s))
        def _(s=s):
            @pl.when(i >= 2)
            def _():
                out_wait(s)
            body(s)

        @pl.when((i == nused) & (i % RING == s))
        def _(s=s):
            in_wait(s)
            in_wait((s + 1) % RING)

            @pl.when(i >= 2)
            def _():
                out_wait(s)
            t = (s + 2) % RING

            @pl.when(i >= 1)
            def _():
                out_wait((s + 1) % RING)

            def last(r, carry):
                scatter_row(prev_ref[0, 0, r], t, r)
                return carry
            lax.fori_loop(0, rows, last, 0)
            out_wait(t)


def _experts_call(blk_e, nused, row_dst, h1t, wgu, bgu, wd, bd):
    nb = row_dst.shape[0] - 4
    rows = EXPERT_ROWS
    n_assign = h1t.shape[0] // 8 * TOP_EXPERTS
    wspec = lambda shape: pl.BlockSpec(shape, lambda i, be, nu: (be[i], 0, 0))
    dst_spec = lambda d: pl.BlockSpec((1, 1, rows), lambda i, be, nu: (i + d, 0, 0), memory_space=pltpu.SMEM)
    grid_spec = pltpu.PrefetchScalarGridSpec(
        num_scalar_prefetch=2,
        grid=(nb + 1,),
        in_specs=[
            dst_spec(0), dst_spec(1), dst_spec(2), dst_spec(3),
            pl.BlockSpec(memory_space=pl.ANY),
            wspec((1, D_MODEL, 2 * D_MODEL)), wspec((1, 1, 2 * D_MODEL)),
            wspec((1, D_MODEL, D_MODEL)), wspec((1, 1, D_MODEL)),
        ],
        out_specs=pl.BlockSpec(memory_space=pl.ANY),
        scratch_shapes=([pltpu.VMEM((8 * rows, LANES), F32)] * RING + [pltpu.VMEM((rows, D_MODEL), BF16)]
                        + [pltpu.VMEM((8 * rows, LANES), F32)] * RING
                        + [pltpu.SemaphoreType.DMA((RING,)), pltpu.SemaphoreType.DMA((RING,))]),
    )
    return pl.pallas_call(
        _experts_kernel,
        grid_spec=grid_spec,
        out_shape=jax.ShapeDtypeStruct((8 * (n_assign + rows), LANES), F32),
        compiler_params=pltpu.CompilerParams(dimension_semantics=("arbitrary",),
                                             vmem_limit_bytes=VMEM_LIMIT),
        name="experts",
    )(blk_e, nused, row_dst, row_dst, row_dst, row_dst, h1t, wgu, bgu, wd, bd)


def _combine_kernel(y0_ref, y1_ref, y2_ref, y3_ref, gate_ref, h_ref, g_ref, b_ref, out_ref):
    n = h_ref.shape[0]
    ffn = gate_ref[:, 0:1] * _from_row_tiles(y0_ref, n)
    for j, y_ref in enumerate((y1_ref, y2_ref, y3_ref), start=1):
        ffn = ffn + gate_ref[:, j:j + 1] * _from_row_tiles(y_ref, n)
    out_ref[...] = _layer_norm(DN_ALPHA * h_ref[...] + ffn, g_ref[...], b_ref[...])


def _combine_call(ys, gates, h1, g, b):
    t = h1.shape[0]
    rows = COMBINE_ROWS
    full = lambda a: pl.BlockSpec(a.shape, lambda i: (0,) * a.ndim)
    choice = lambda j: pl.BlockSpec((8 * rows, LANES), lambda i: (j * (t // rows) + i, 0))
    return pl.pallas_call(
        _combine_kernel,
        grid=(t // rows,),
        in_specs=[
            choice(0), choice(1), choice(2), choice(3),
            pl.BlockSpec((rows, LANES), lambda i: (i, 0)),
            pl.BlockSpec((rows, D_MODEL), lambda i: (i, 0)),
            full(g), full(b),
        ],
        out_specs=pl.BlockSpec((rows, D_MODEL), lambda i: (i, 0)),
        out_shape=jax.ShapeDtypeStruct((t, D_MODEL), F32),
        compiler_params=pltpu.CompilerParams(dimension_semantics=("arbitrary",),
                                             vmem_limit_bytes=VMEM_LIMIT),
        name="combine",
    )(ys, ys, ys, ys, gates, h1, g, b)


def _routing_tables(top_e, counts):
    t = top_e.shape[0]
    a = t * TOP_EXPERTS
    rows = EXPERT_ROWS
    assert t & (t - 1) == 0 and a <= (1 << ASSIGN_BITS)
    ids = jnp.arange(a, dtype=I32)
    keys = jnp.sort((top_e.T.reshape(a) << ASSIGN_BITS) | ids)
    nblk = (counts + rows - 1) // rows
    blk_end = jnp.cumsum(nblk)
    start = jnp.cumsum(counts) - counts
    nb = a // rows + N_EXPERTS
    nused = blk_end[-1].astype(I32)
    b = jnp.arange(nb + 3, dtype=I32)
    used = b < nused
    be = jnp.sum((jnp.minimum(b, nused - 1)[:, None] >= blk_end[None, :]).astype(I32), axis=1)
    onehot = (be[:, None] == jnp.arange(N_EXPERTS, dtype=I32)[None, :]).astype(I32)
    pick = lambda v: jnp.sum(onehot * v[None, :], axis=1)
    within = (b - (pick(blk_end) - pick(nblk))) * rows
    blk_start = jnp.where(used, pick(start) + within, 0).astype(I32)
    blk_n = jnp.where(used, jnp.clip(pick(counts) - within, 0, rows), 0).astype(I32)
    r = jnp.arange(rows, dtype=I32)[None, :]
    window = jnp.minimum(blk_start[:, None] + r, a - 1)
    dump = jnp.broadcast_to(a + r, (nb + 3, rows))
    row_dst = jnp.where(r < blk_n[:, None], keys[window] & ((1 << ASSIGN_BITS) - 1), dump)
    row_dst = jnp.concatenate([dump[:1], row_dst], axis=0).reshape(nb + 4, 1, rows)
    return be.astype(I32), nused.reshape(1), row_dst


def kernel(x, w_in, kv_norm_g, w_uk, w_uv, m_conv_w, m_conv_b, w_mq, w_mk, b_i, b_f, m_norm_g, w_out,
           ln1_g, ln1_b, w_router, b_router, w_gu, b_gu, w_down, b_down, ln2_g, ln2_b):
    bsz, seq, _ = x.shape
    assert seq == SEQ
    t = bsz * seq
    x2 = x.reshape(t, D_MODEL)
    w = w_in[0]
    o_qa, o_ckv, o_qi, o_ki, o_wi, o_u, o_v, o_z, o_i, o_f, o_end = 0, 512, 640, 1152, 1216, 1224, 1736, 2248, 2760, 2764, 2768
    wmain = jnp.concatenate([w[:, o_qa:o_ki], w[:, o_u:o_i]], axis=1).astype(BF16)
    wsmall = jnp.zeros((D_MODEL, LANES), F32).at[:, 0:72].set(w[:, o_ki:o_u]).at[:, 72:80].set(w[:, o_i:o_end]).astype(BF16)
    wsmallt = (jnp.zeros((LANES, D_MODEL), F32).at[0:64].set(w[:, o_ki:o_wi].T).at[64:72].set(w[:, o_i:o_end].T)
               .at[72:80].set(w[:, o_wi:o_u].T).astype(BF16))
    wckvt = w[:, o_ckv:o_qi].T.astype(BF16)
    wukt = jnp.swapaxes(w_uk[0], 1, 2).astype(BF16)

    qlat, c, ct, qidx, small, smallt, qm, km, v, z = _proj_call(
        x2, wmain, wsmall, wsmallt, wckvt, wukt, kv_norm_g[0].reshape(1, KV_RANK),
        kv_norm_g[0].reshape(KV_RANK, 1), m_conv_w[0],
        m_conv_b[0].reshape(1, M_WIDTH), w_mq[0].astype(BF16), w_mk[0].astype(BF16))

    assert A_HEADS == 8
    aout = _dsa_call(qidx.reshape(bsz, seq, IDX_HEADS * IDX_DIM), smallt,
                     qlat.reshape(bsz, seq, A_HEADS * KV_RANK), small.reshape(bsz, seq, LANES),
                     c.reshape(bsz, seq, KV_RANK), ct, w_uv[0].astype(BF16))

    grow = smallt[64:72].reshape(8, bsz, N_CHUNKS, CHUNK).transpose(1, 2, 0, 3)
    bias8 = jnp.concatenate([b_i[0], b_f[0]])
    bcol = jnp.zeros((1, LANES), F32).at[0, 72:80].set(bias8)
    mout = _mlstm_call(qm.reshape(bsz, seq, M_WIDTH), km.reshape(bsz, seq, M_WIDTH),
                       v.reshape(bsz, seq, M_WIDTH), z.reshape(bsz, seq, M_WIDTH),
                       small.reshape(bsz, seq, LANES), grow, bcol, bias8.reshape(8, 1),
                       m_norm_g[0].reshape(1, M_WIDTH))

    wr = jnp.zeros((D_MODEL, LANES), F32).at[:, :N_EXPERTS].set(w_router[0]).astype(BF16)
    br = jnp.full((1, LANES), NEG_BIG, F32).at[0, :N_EXPERTS].set(b_router[0])
    h1, h1t, e_slab, g_slab, cnt = _mix_call(aout.reshape(t, A_WIDTH), mout.reshape(t, M_WIDTH), x2,
                                        w_out[0].astype(BF16), ln1_g[0].reshape(1, D_MODEL),
                                        ln1_b[0].reshape(1, D_MODEL), wr, br)

    blk_e, nused, row_dst = _routing_tables(e_slab[:, :TOP_EXPERTS], cnt[0, :N_EXPERTS].astype(I32))
    ys = _experts_call(blk_e, nused, row_dst, h1t, w_gu[0].astype(BF16),
                       b_gu[0].reshape(N_EXPERTS, 1, 2 * D_MODEL), w_down[0].astype(BF16),
                       b_down[0].reshape(N_EXPERTS, 1, D_MODEL))
    out = _combine_call(ys, g_slab, h1, ln2_g[0].reshape(1, D_MODEL), ln2_b[0].reshape(1, D_MODEL))
    return out.reshape(bsz, seq, D_MODEL)
```

```python
import jax
import jax.numpy as jnp
from jax import lax
from jax.experimental import pallas as pl
from jax.experimental.pallas import tpu as pltpu

F32 = jnp.float32
BF16 = jnp.bfloat16
I32 = jnp.int32

D_MODEL = 1024
SEQ = 2048
CHUNK = 64
A_HEADS = 8
A_HEAD_DIM = 64
KV_RANK = 128
IDX_HEADS = 8
IDX_DIM = 64
K_SEL = 256
M_HEADS = 4
M_HEAD_DIM = 128
CONV_WIDTH = 4
N_EXPERTS = 32
TOP_EXPERTS = 4
SWIGLU_LIMIT = 7.0
SWIGLU_ALPHA = 1.702
LN_EPS = 1e-5
DN_ALPHA = 2.0 ** 0.25

A_WIDTH = A_HEADS * A_HEAD_DIM
M_WIDTH = M_HEADS * M_HEAD_DIM

LANES = 128
NEG_BIG = -1e30
INT_MIN = -(2 ** 31)

PROJ_ROWS = 512
KEY_BLOCK = 256
EXPERT_ROWS = 256
COMBINE_ROWS = 256
VMEM_LIMIT = 56 * 1024 * 1024

_QA, _CKV, _QI, _U, _V, _Z = 0, 512, 640, 1152, 1664, 2176
MAIN_WIDTH = 2688


def _dot(a, b):
    return jnp.dot(a, b, preferred_element_type=F32)


def _dot_nt(a, b):
    return lax.dot_general(a, b, (((1,), (1,)), ((), ())), preferred_element_type=F32)


def _proj_kernel(x_ref, wmain_ref, wsmall_ref, wsmallt_ref, wckvt_ref, wukt_ref, kvg_ref, kvgcol_ref,
                 convw_ref, convb_ref, wmq_ref, wmk_ref,
                 qlat_ref, c_ref, ct_ref, qidx_ref, small_ref, smallt_ref, qmt_ref, km_ref, vt_ref, zt_ref,
                 uext_ref):
    i = pl.program_id(0)
    rows = x_ref.shape[0]
    xb = x_ref[...].astype(BF16)

    qa = _dot(xb, wmain_ref[:, _QA:_CKV])
    for h in range(A_HEADS):
        qh = qa[:, h * A_HEAD_DIM:(h + 1) * A_HEAD_DIM].astype(BF16)
        ql = _dot(qh, wukt_ref[h]) * (A_HEAD_DIM ** -0.5)
        qlat_ref[:, h * KV_RANK:(h + 1) * KV_RANK] = ql.astype(BF16)

    ckv = _dot(xb, wmain_ref[:, _CKV:_QI])
    ms = jnp.mean(ckv * ckv, axis=-1, keepdims=True)
    c_ref[...] = (ckv * lax.rsqrt(ms + LN_EPS) * kvg_ref[...]).astype(BF16)
    ckv_t = _dot_nt(wckvt_ref[...], xb)
    ms_t = jnp.mean(ckv_t * ckv_t, axis=0, keepdims=True)
    ct_ref[...] = (ckv_t * lax.rsqrt(ms_t + LN_EPS) * kvgcol_ref[...]).astype(BF16)

    qidx_ref[...] = _dot(xb, wmain_ref[:, _QI:_U]).astype(BF16)
    vt_ref[...] = _dot(xb, wmain_ref[:, _V:_Z]).T.astype(BF16)
    zt_ref[...] = _dot(xb, wmain_ref[:, _Z:MAIN_WIDTH]).T
    small_ref[...] = _dot(xb, wsmall_ref[...])
    smallt_ref[...] = _dot_nt(wsmallt_ref[...], xb)

    u = _dot(xb, wmain_ref[:, _U:_V])

    @pl.when(i % (SEQ // rows) == 0)
    def _():
        uext_ref[0:8, :] = jnp.zeros((8, M_WIDTH), F32)

    uext_ref[8:8 + rows, :] = u
    acc = jnp.broadcast_to(convb_ref[...], (rows, M_WIDTH))
    for j in range(CONV_WIDTH):
        off = 8 - (CONV_WIDTH - 1) + j
        acc = acc + convw_ref[j:j + 1, :] * uext_ref[off:off + rows, :]
    uext_ref[0:8, :] = u[rows - 8:rows, :]
    uc = acc * jax.nn.sigmoid(acc)
    for h in range(M_HEADS):
        sl = slice(h * M_HEAD_DIM, (h + 1) * M_HEAD_DIM)
        uh = uc[:, sl].astype(BF16)
        qmt_ref[sl, :] = _dot(uh, wmq_ref[h]).T.astype(BF16)
        km_ref[:, sl] = (_dot(uh, wmk_ref[h]) * (M_HEAD_DIM ** -0.5)).astype(BF16)


def _proj_call(x2, wmain, wsmall, wsmallt, wckvt, wukt, kvg, kvgcol, convw, convb, wmq, wmk):
    t = x2.shape[0]
    rows = PROJ_ROWS
    full = lambda a: pl.BlockSpec(a.shape, lambda i: (0,) * a.ndim)
    tok = lambda w: pl.BlockSpec((rows, w), lambda i: (i, 0))
    tok_t = pl.BlockSpec((LANES, rows), lambda i: (0, i))
    out_shape = (
        jax.ShapeDtypeStruct((t, A_HEADS * KV_RANK), BF16),
        jax.ShapeDtypeStruct((t, KV_RANK), BF16),
        jax.ShapeDtypeStruct((KV_RANK, t), BF16),
        jax.ShapeDtypeStruct((t, IDX_HEADS * IDX_DIM), BF16),
        jax.ShapeDtypeStruct((t, LANES), F32),
        jax.ShapeDtypeStruct((LANES, t), F32),
        jax.ShapeDtypeStruct((M_WIDTH, t), BF16),
        jax.ShapeDtypeStruct((t, M_WIDTH), BF16),
        jax.ShapeDtypeStruct((M_WIDTH, t), BF16),
        jax.ShapeDtypeStruct((M_WIDTH, t), F32),
    )
    wide_t = pl.BlockSpec((M_WIDTH, rows), lambda i: (0, i))
    out_specs = (tok(A_HEADS * KV_RANK), tok(KV_RANK), tok_t, tok(IDX_HEADS * IDX_DIM), tok(LANES), tok_t,
                 wide_t, tok(M_WIDTH), wide_t, wide_t)
    return pl.pallas_call(
        _proj_kernel,
        grid=(t // rows,),
        in_specs=[tok(D_MODEL), full(wmain), full(wsmall), full(wsmallt), full(wckvt), full(wukt), full(kvg),
                  full(kvgcol), full(convw), full(convb), full(wmq), full(wmk)],
        out_specs=out_specs,
        out_shape=out_shape,
        scratch_shapes=[pltpu.VMEM((8 + rows, M_WIDTH), F32)],
        compiler_params=pltpu.CompilerParams(dimension_semantics=("arbitrary",),
                                             vmem_limit_bytes=VMEM_LIMIT),
        name="proj",
    )(x2, wmain, wsmall, wsmallt, wckvt, wukt, kvg, kvgcol, convw, convb, wmq, wmk)


QUERY_TILE = 2 * CHUNK
N_QTILES = SEQ // QUERY_TILE
N_KEY_BLOCKS = SEQ // KEY_BLOCK


def _key_positions(kb):
    return kb * KEY_BLOCK + lax.broadcasted_iota(I32, (KEY_BLOCK, QUERY_TILE), 0)


def _count_keys(sc_ref, nkb, pred):
    def body(kb, acc):
        hit = jnp.where(pred(sc_ref[kb], _key_positions(kb)), 1.0, 0.0)
        return acc + jnp.sum(hit.reshape(KEY_BLOCK // 32, 32, QUERY_TILE), axis=0)
    acc = lax.fori_loop(0, nkb, body, jnp.zeros((32, QUERY_TILE), F32))
    return jnp.sum(acc, axis=0, keepdims=True)


def _dsa_kernel(qidx_ref, wt_ref, qlat_ref, small_ref, c_ref, ct_ref, wuv_ref,
                aout_ref,
                kidx_scr, ctb_scr, sc_ref, thr_ref, jlim_ref, m_ref, l_ref, acc_ref):
    qt = pl.program_id(1)

    @pl.when(qt == 0)
    def _():
        kidx_scr[...] = small_ref[0, :, 0:IDX_DIM].astype(BF16)
        for kb in range(N_KEY_BLOCKS):
            ctb_scr[kb] = ct_ref[:, kb * KEY_BLOCK:(kb + 1) * KEY_BLOCK]

    lane = lax.broadcasted_iota(I32, (1, QUERY_TILE), 1)
    qpos = qt * QUERY_TILE + lane
    key_limit = (qt * 2 + 1 + jnp.where(lane >= CHUNK, 1, 0)) * CHUNK
    nkb = (qt + 2) // 2

    qi2 = [jnp.concatenate([qidx_ref[0, :, h * IDX_DIM:(h + 1) * IDX_DIM] for h in (2 * g, 2 * g + 1)], axis=0)
           for g in range(IDX_HEADS // 2)]
    wt = wt_ref[...]

    def score_body(kb, carry):
        kblk = kidx_scr[pl.ds(pl.multiple_of(kb * KEY_BLOCK, KEY_BLOCK), KEY_BLOCK), :]
        sc = None
        for g in range(IDX_HEADS // 2):
            rel = jnp.maximum(_dot_nt(kblk, qi2[g]), 0.0)
            for k in range(2):
                h = 2 * g + k
                term = rel[:, k * QUERY_TILE:(k + 1) * QUERY_TILE] * wt[h:h + 1, :]
                sc = term if sc is None else sc + term
        sc_ref[kb] = jnp.where(_key_positions(kb) < key_limit, sc, -jnp.inf)
        return carry

    lax.fori_loop(0, nkb, score_body, 0)

    thr_ref[...] = jnp.full(thr_ref.shape, -jnp.inf, F32)
    jlim_ref[...] = jnp.full(jlim_ref.shape, -1, I32)

    def as_score(code):
        return pltpu.bitcast(jnp.where(code < 0, code ^ jnp.int32(0x7FFFFFFF), code), F32)

    @pl.when(qt * QUERY_TILE + CHUNK > K_SEL)
    def _():
        k_sel = jnp.float32(K_SEL)
        cnt0 = _count_keys(sc_ref, nkb, lambda sc, pos: sc >= 0.0)
        t0 = jnp.where(cnt0 >= k_sel, jnp.int32(0), jnp.int32(INT_MIN))

        def bit_body(p, t):
            cand = t + lax.shift_left(jnp.int32(1), jnp.int32(30) - p)
            cand_f = as_score(cand)
            cnt = _count_keys(sc_ref, nkb, lambda sc, pos: sc >= cand_f)
            return jnp.where(cnt >= k_sel, cand, t)

        t = as_score(lax.fori_loop(0, 31, bit_body, t0))
        thr_ref[...] = jnp.broadcast_to(t, thr_ref.shape)
        jlim_ref[...] = jnp.full(jlim_ref.shape, SEQ - 1, I32)
        cnt_ge = _count_keys(sc_ref, nkb, lambda sc, pos: sc >= t)

        @pl.when(jnp.max(cnt_ge) > k_sel)
        def _():
            need = k_sel - _count_keys(sc_ref, nkb, lambda sc, pos: sc > t)

            def pos_body(p, j):
                cand = j + lax.shift_left(jnp.int32(1), jnp.int32(10) - p)
                cnt = _count_keys(sc_ref, nkb, lambda sc, pos: (sc == t) & (pos < cand))
                return jnp.where(cnt < need, cand, j)

            j = lax.fori_loop(0, 11, pos_body, jnp.zeros((1, QUERY_TILE), I32))
            jlim_ref[...] = jnp.broadcast_to(j, jlim_ref.shape)

    thr = thr_ref[0:1, :]
    jlim = jlim_ref[0:1, :]
    qlat_all = jnp.concatenate(
        [qlat_ref[0, :, h * KV_RANK:(h + 1) * KV_RANK] for h in range(A_HEADS)], axis=0)
    m_ref[...] = jnp.full(m_ref.shape, NEG_BIG, F32)
    l_ref[...] = jnp.zeros(l_ref.shape, F32)
    acc_ref[...] = jnp.zeros(acc_ref.shape, F32)

    def att_body(kb, carry):
        cblk = c_ref[0, pl.ds(pl.multiple_of(kb * KEY_BLOCK, KEY_BLOCK), KEY_BLOCK), :]
        s_all = _dot_nt(cblk, qlat_all)
        sc = sc_ref[kb]
        kpos = _key_positions(kb)
        sel = (sc > thr) | ((sc == thr) & (kpos <= jlim))
        bias = jnp.where(sel, 0.0, NEG_BIG)
        negdist = -jnp.abs(qpos - kpos).astype(F32)
        m_all = m_ref[...]
        l_all = l_ref[...]
        ps, alphas, ms, ls = [], [], [], []
        for h in range(A_HEADS):
            s = s_all[:, h * QUERY_TILE:(h + 1) * QUERY_TILE] + (2.0 ** -(h + 1)) * negdist + bias
            m_old = m_all[h:h + 1, :]
            m_new = jnp.maximum(m_old, jnp.max(s, axis=0, keepdims=True))
            alpha = jnp.exp(m_old - m_new)
            p = jnp.exp(s - m_new)
            ls.append(alpha * l_all[h:h + 1, :] + jnp.sum(p, axis=0, keepdims=True))
            ms.append(m_new)
            ps.append(p.astype(BF16))
            alphas.append(alpha)
        m_ref[...] = jnp.concatenate(ms, axis=0)
        l_ref[...] = jnp.concatenate(ls, axis=0)
        pv = _dot(ctb_scr[kb], jnp.concatenate(ps, axis=1))
        acc_ref[...] = acc_ref[...] * jnp.concatenate(alphas, axis=1) + pv
        return carry

    lax.fori_loop(0, nkb, att_body, 0)
    outs = []
    for h in range(A_HEADS):
        sl = slice(h * QUERY_TILE, (h + 1) * QUERY_TILE)
        o_t = (acc_ref[:, sl] / l_ref[h:h + 1, :]).astype(BF16)
        outs.append(lax.dot_general(o_t, wuv_ref[h], (((0,), (0,)), ((), ())), preferred_element_type=F32))
    aout_ref[0] = jnp.concatenate(outs, axis=-1).astype(BF16)


def _dsa_call(qidx, smallt, qlat, small, c, ct, wuv):
    b = qlat.shape[0]
    return pl.pallas_call(
        _dsa_kernel,
        grid=(b, N_QTILES),
        in_specs=[
            pl.BlockSpec((1, QUERY_TILE, IDX_HEADS * IDX_DIM), lambda bi, qt: (bi, qt, 0)),
            pl.BlockSpec((8, QUERY_TILE), lambda bi, qt: (9, bi * N_QTILES + qt)),
            pl.BlockSpec((1, QUERY_TILE, A_HEADS * KV_RANK), lambda bi, qt: (bi, qt, 0)),
            pl.BlockSpec((1, SEQ, LANES), lambda bi, qt: (bi, 0, 0)),
            pl.BlockSpec((1, SEQ, KV_RANK), lambda bi, qt: (bi, 0, 0)),
            pl.BlockSpec((KV_RANK, SEQ), lambda bi, qt: (0, bi)),
            pl.BlockSpec(wuv.shape, lambda bi, qt: (0, 0, 0)),
        ],
        out_specs=pl.BlockSpec((1, QUERY_TILE, A_WIDTH), lambda bi, qt: (bi, qt, 0)),
        out_shape=jax.ShapeDtypeStruct((b, SEQ, A_WIDTH), BF16),
        scratch_shapes=[pltpu.VMEM((SEQ, IDX_DIM), BF16),
                        pltpu.VMEM((N_KEY_BLOCKS, KV_RANK, KEY_BLOCK), BF16),
                        pltpu.VMEM((N_KEY_BLOCKS, KEY_BLOCK, QUERY_TILE), F32),
                        pltpu.VMEM((8, QUERY_TILE), F32), pltpu.VMEM((8, QUERY_TILE), I32),
                        pltpu.VMEM((A_HEADS, QUERY_TILE), F32), pltpu.VMEM((A_HEADS, QUERY_TILE), F32),
                        pltpu.VMEM((KV_RANK, A_HEADS * QUERY_TILE), F32)],
        compiler_params=pltpu.CompilerParams(dimension_semantics=("arbitrary", "arbitrary"),
                                             vmem_limit_bytes=VMEM_LIMIT),
        name="dsa",
    )(qidx, smallt, qlat, small, c, ct, wuv)


def _log_sigmoid(x):
    return jnp.minimum(x, 0.0) - jnp.log(1.0 + jnp.exp(-jnp.abs(x)))


MCHUNK = 128
N_MCHUNKS = SEQ // MCHUNK
MLSTM_GROUP = 2


def _mlstm_kernel(*refs):
    grp = (len(refs) - 9) // 4
    k_ref, gcol_ref = refs[0], refs[1]
    qt_refs = refs[2:2 + grp]
    vt_refs = refs[2 + grp:2 + 2 * grp]
    zt_refs = refs[2 + 2 * grp:2 + 3 * grp]
    grow_refs = refs[2 + 3 * grp:2 + 4 * grp]
    bcol_ref, brow_ref, normg_ref, out_ref, c_ref, n_ref, m_ref = refs[2 + 4 * grp:]
    ci = pl.program_id(1)

    @pl.when(ci == 0)
    def _():
        c_ref[...] = jnp.zeros(c_ref.shape, F32)
        n_ref[...] = jnp.zeros(n_ref.shape, F32)
        m_ref[...] = jnp.full(m_ref.shape, NEG_BIG, F32)

    hi = lax.Precision.HIGHEST
    r_i = lax.broadcasted_iota(I32, (MCHUNK, MCHUNK), 0)
    c_i = lax.broadcasted_iota(I32, (MCHUNK, MCHUNK), 1)
    tril = jnp.where(c_i <= r_i, 1.0, 0.0).astype(F32)
    triu = jnp.where(r_i <= c_i, 1.0, 0.0).astype(F32)
    src_before = r_i <= c_i

    for bb, h in [(bb, h) for bb in range(grp) for h in range(M_HEADS)]:
        if h == 0:
            gcol = gcol_ref[bb] + bcol_ref[...]
            grow = grow_refs[bb][...] + brow_ref[...]
            bcum_col = jnp.dot(tril, _log_sigmoid(gcol), precision=hi, preferred_element_type=F32)
            bcum_row = jnp.dot(_log_sigmoid(grow), triu, precision=hi, preferred_element_type=F32)
        st = bb * M_HEADS + h
        sl = slice(h * M_HEAD_DIM, (h + 1) * M_HEAD_DIM)
        k = k_ref[bb, :, sl]
        q_t = qt_refs[bb][sl, :]
        v_t = vt_refs[bb][sl, :]
        li_row = grow[h:h + 1, :]
        b_row = bcum_row[4 + h:5 + h, :]
        colv = gcol[:, 72 + h:73 + h] - bcum_col[:, 76 + h:77 + h]
        b_tot = b_row[:, MCHUNK - 1:MCHUNK]
        c_prev = c_ref[st]
        n_prev = n_ref[st]
        m_prev = m_ref[st][:, 0:1]

        g = b_row + m_prev
        dlog = jnp.where(src_before, b_row + colv, NEG_BIG)
        m_t = jnp.maximum(g, jnp.max(dlog, axis=0, keepdims=True))
        w_intra = jnp.exp(dlog - m_t)
        w_state = jnp.exp(g - m_t)
        s_t = _dot(k, q_t) * w_intra
        num = _dot(v_t, s_t.astype(BF16)) + w_state * _dot(c_prev.astype(BF16), q_t)
        den = jnp.sum(s_t, axis=0, keepdims=True) + w_state * _dot(n_prev.astype(BF16), q_t)[0:1, :]
        hc = num / jnp.maximum(jnp.abs(den), jnp.exp(-m_t))
        mu = jnp.mean(hc, axis=0, keepdims=True)
        var = jnp.mean(jnp.square(hc - mu), axis=0, keepdims=True)
        hc = (hc - mu) * lax.rsqrt(var + LN_EPS)
        o_t = jax.nn.sigmoid(zt_refs[bb][sl, :]) * (hc * normg_ref[sl, :])
        out_ref[bb, :, sl] = o_t.T.astype(BF16)

        a_row = b_tot - b_row + li_row
        m_new = jnp.maximum(b_tot + m_prev, jnp.max(a_row, axis=-1, keepdims=True))
        decay = jnp.exp(b_tot + m_prev - m_new)
        w_row = jnp.exp(a_row - m_new)
        c_ref[st] = decay * c_prev + _dot((v_t.astype(F32) * w_row).astype(BF16), k)
        n_ref[st] = decay * n_prev + _dot(jnp.broadcast_to(w_row, (8, MCHUNK)).astype(BF16), k)
        m_ref[st] = jnp.broadcast_to(m_new, m_ref.shape[1:])


def _mlstm_call(km, small, qmt, vt, zt, smallt, bcol, brow, normgb):
    b = km.shape[0]
    grp = MLSTM_GROUP if b % MLSTM_GROUP == 0 else 1
    tokspec = lambda w: pl.BlockSpec((grp, MCHUNK, w), lambda bi, ci: (bi, ci, 0))
    full = lambda a: pl.BlockSpec(a.shape, lambda bi, ci: (0,) * a.ndim)
    lanes = lambda rows, row_blk, bb: pl.BlockSpec(
        (rows, MCHUNK), lambda bi, ci: (row_blk, (bi * grp + bb) * N_MCHUNKS + ci))
    per_seq = lambda rows, row_blk: [lanes(rows, row_blk, bb) for bb in range(grp)]
    return pl.pallas_call(
        _mlstm_kernel,
        grid=(b // grp, N_MCHUNKS),
        in_specs=([tokspec(M_WIDTH), tokspec(LANES)] + per_seq(M_WIDTH, 0) + per_seq(M_WIDTH, 0)
                  + per_seq(M_WIDTH, 0) + per_seq(8, 8) + [full(bcol), full(brow), full(normgb)]),
        out_specs=tokspec(M_WIDTH),
        out_shape=jax.ShapeDtypeStruct((b, SEQ, M_WIDTH), BF16),
        scratch_shapes=[pltpu.VMEM((grp * M_HEADS, M_HEAD_DIM, M_HEAD_DIM), F32),
                        pltpu.VMEM((grp * M_HEADS, 8, M_HEAD_DIM), F32),
                        pltpu.VMEM((grp * M_HEADS, 1, LANES), F32)],
        compiler_params=pltpu.CompilerParams(dimension_semantics=("arbitrary", "arbitrary"),
                                             vmem_limit_bytes=VMEM_LIMIT),
        name="mlstm",
    )(km, small, *([qmt] * grp), *([vt] * grp), *([zt] * grp), *([smallt] * grp), bcol, brow, normgb)


def _layer_norm(y, g, b):
    mu = jnp.mean(y, axis=-1, keepdims=True)
    var = jnp.mean(jnp.square(y - mu), axis=-1, keepdims=True)
    return (y - mu) * lax.rsqrt(var + LN_EPS) * g + b


ROUTER_SLAB = 64


def _to_row_tiles(ref, val, row0=0):
    n = val.shape[0]
    for k in range(D_MODEL // LANES):
        ref[pl.ds(8 * row0 + k, n, stride=8), :] = val[:, k * LANES:(k + 1) * LANES]


def _from_row_tiles(ref, n):
    return jnp.concatenate([ref[pl.ds(k, n, stride=8), :] for k in range(D_MODEL // LANES)], axis=1)


def _mix_kernel(a_ref, m_ref, x_ref, wout_ref, g_ref, b_ref, wr_ref, br_ref,
                h_ref, ht_ref, e_ref, gate_ref, cnt_ref):
    @pl.when(pl.program_id(0) == 0)
    def _():
        cnt_ref[...] = jnp.zeros(cnt_ref.shape, F32)

    mix = _dot(a_ref[...], wout_ref[0:A_WIDTH, :]) + _dot(m_ref[...], wout_ref[A_WIDTH:, :])
    h1 = _layer_norm(DN_ALPHA * x_ref[...] + mix, g_ref[...], b_ref[...])
    h_ref[...] = h1
    _to_row_tiles(ht_ref, h1)
    logits_all = _dot(h1.astype(BF16), wr_ref[...]) + br_ref[...]
    slab = ROUTER_SLAB
    lane = lax.broadcasted_iota(I32, (slab, LANES), 1).astype(F32)
    cnt = jnp.zeros((1, LANES), F32)
    for r0 in range(0, logits_all.shape[0], slab):
        logits = logits_all[r0:r0 + slab]
        e_slab = jnp.zeros((slab, LANES), F32)
        v_slab = jnp.zeros((slab, LANES), F32)
        chosen = jnp.zeros((slab, LANES), F32)
        vals = []
        for j in range(TOP_EXPERTS):
            mx = jnp.max(logits, axis=-1, keepdims=True)
            idx = jnp.min(jnp.where(logits == mx, lane, float(LANES)), axis=-1, keepdims=True)
            e_slab = jnp.where(lane == float(j), idx, e_slab)
            vals.append(mx)
            chosen = jnp.where(lane == idx, 1.0, chosen)
            logits = jnp.where(lane == idx, -jnp.inf, logits)
        cnt = cnt + jnp.sum(chosen, axis=0, keepdims=True)
        ex = [jnp.exp(vj - vals[0]) for vj in vals]
        tot = ex[0] + ex[1] + ex[2] + ex[3]
        for j in range(TOP_EXPERTS):
            v_slab = jnp.where(lane == float(j), ex[j] / tot, v_slab)
        e_ref[r0:r0 + slab, :] = e_slab.astype(I32)
        gate_ref[r0:r0 + slab, :] = v_slab
    cnt_ref[...] += cnt


def _mix_call(aout, mout, x2, wout, g, b, wr, br):
    t = x2.shape[0]
    rows = PROJ_ROWS
    full = lambda a: pl.BlockSpec(a.shape, lambda i: (0,) * a.ndim)
    tok = lambda w: pl.BlockSpec((rows, w), lambda i: (i, 0))
    return pl.pallas_call(
        _mix_kernel,
        grid=(t // rows,),
        in_specs=[tok(A_WIDTH), tok(M_WIDTH), tok(D_MODEL), full(wout), full(g), full(b), full(wr), full(br)],
        out_specs=(tok(D_MODEL), pl.BlockSpec((8 * rows, LANES), lambda i: (i, 0)), tok(LANES), tok(LANES),
                   pl.BlockSpec((1, LANES), lambda i: (0, 0))),
        out_shape=(jax.ShapeDtypeStruct((t, D_MODEL), F32),
                   jax.ShapeDtypeStruct((8 * t, LANES), F32),
                   jax.ShapeDtypeStruct((t, LANES), I32),
                   jax.ShapeDtypeStruct((t, LANES), F32),
                   jax.ShapeDtypeStruct((1, LANES), F32)),
        compiler_params=pltpu.CompilerParams(dimension_semantics=("arbitrary",),
                                             vmem_limit_bytes=VMEM_LIMIT),
        name="mix",
    )(aout, mout, x2, wout, g, b, wr, br)


ASSIGN_BITS = 18
FFN_SECTIONS = 8
HID_CHUNK = D_MODEL // 4
RING = 3


def _experts_kernel(blk_e_ref, nused_ref,
                    prev_ref, cur_ref, next_ref, next2_ref, h_hbm, wgu_ref, bgu_ref, wd_ref, bd_ref,
                    ys_hbm,
                    xbuf0_ref, xbuf1_ref, xbuf2_ref, hbuf_ref, ybuf0_ref, ybuf1_ref, ybuf2_ref, wgub_ref, wdb_ref,
                    in_sem, out_sem):
    i = pl.program_id(0)
    nused = nused_ref[0]
    rows = EXPERT_ROWS
    n_tok = h_hbm.shape[0] // 8
    per_sec = rows // FFN_SECTIONS
    xbuf = (xbuf0_ref, xbuf1_ref, xbuf2_ref)
    ybuf = (ybuf0_ref, ybuf1_ref, ybuf2_ref)

    def in_wait(s):
        pltpu.make_async_copy(h_hbm.at[pl.ds(0, 8 * rows)], xbuf[s], in_sem.at[s]).wait()

    def out_wait(s):
        pltpu.make_async_copy(ybuf[s], ys_hbm.at[pl.ds(0, 8 * rows)], out_sem.at[s]).wait()

    def tile(r):
        return pl.ds(r * 8, 8) if isinstance(r, int) else pl.ds(pl.multiple_of(r * 8, 8), 8)

    def prio(r):
        return r % 2 if isinstance(r, int) else 0

    def gather_row(a, s, r):
        tok = a & (n_tok - 1)
        pltpu.make_async_copy(h_hbm.at[tile(tok)], xbuf[s].at[tile(r)], in_sem.at[s]).start(priority=prio(r))

    def scatter_row(a, s, r):
        pltpu.make_async_copy(ybuf[s].at[tile(r)], ys_hbm.at[tile(a)], out_sem.at[s]).start(priority=1 - prio(r))

    @pl.when(i == 0)
    def _():
        ybuf2_ref[...] = jnp.zeros(ybuf2_ref.shape, F32)

        def first(r, carry):
            gather_row(cur_ref[0, 0, r], 0, r)
            gather_row(next_ref[0, 0, r], 1, r)
            return carry
        lax.fori_loop(0, rows, first, 0)

    def moves(s, lo, hi):
        t = (s + 2) % RING
        for r in range(lo, hi):
            scatter_row(prev_ref[0, 0, r], t, r)
            gather_row(next2_ref[0, 0, r], t, r)

    def body(s):
        @pl.when((i == 0) | (blk_e_ref[i] != blk_e_ref[jnp.maximum(i - 1, 0)]))
        def _():
            wgub_ref[...] = wgu_ref[0].astype(BF16)
            wdb_ref[...] = wd_ref[0].astype(BF16)

        in_wait(s)
        xb = _from_row_tiles(xbuf[s], rows).astype(BF16)
        sec = 0
        for c in range(D_MODEL // HID_CHUNK):
            lo, hi = c * HID_CHUNK, (c + 1) * HID_CHUNK
            g = _dot(xb, wgub_ref[:, lo:hi]) + bgu_ref[0, :, lo:hi]
            u = _dot(xb, wgub_ref[:, D_MODEL + lo:D_MODEL + hi]) + bgu_ref[0, :, D_MODEL + lo:D_MODEL + hi]
            g = jnp.minimum(g, SWIGLU_LIMIT)
            u = jnp.clip(u, -SWIGLU_LIMIT, SWIGLU_LIMIT)
            hbuf_ref[:, lo:hi] = ((u + 1.0) * g * jax.nn.sigmoid(SWIGLU_ALPHA * g)).astype(BF16)
            moves(s, sec * per_sec, (sec + 1) * per_sec)
            sec += 1
        hid = hbuf_ref[...]
        for c in range(D_MODEL // HID_CHUNK):
            lo, hi = c * HID_CHUNK, (c + 1) * HID_CHUNK
            y = _dot(hid, wdb_ref[:, lo:hi]) + bd_ref[0, :, lo:hi]
            for k in range(HID_CHUNK // LANES):
                ybuf[s][pl.ds(lo // LANES + k, rows, stride=8), :] = y[:, k * LANES:(k + 1) * LANES]
            moves(s, sec * per_sec, (sec + 1) * per_sec)
            sec += 1

    for s in range(RING):
        @pl.when((i < nused) & (i % RING == s))
        def _(s=s):
            @pl.when(i >= 2)
            def _():
                out_wait(s)
            body(s)

        @pl.when((i == nused) & (i % RING == s))
        def _(s=s):
            in_wait(s)
            in_wait((s + 1) % RING)

            @pl.when(i >= 2)
            def _():
                out_wait(s)
            t = (s + 2) % RING

            @pl.when(i >= 1)
            def _():
                out_wait((s + 1) % RING)

            def last(r, carry):
                scatter_row(prev_ref[0, 0, r], t, r)
                return carry
            lax.fori_loop(0, rows, last, 0)
            out_wait(t)


def _experts_call(blk_e, nused, row_dst, h1t, wgu, bgu, wd, bd):
    nb = row_dst.shape[0] - 4
    rows = EXPERT_ROWS
    n_assign = h1t.shape[0] // 8 * TOP_EXPERTS
    wspec = lambda shape: pl.BlockSpec(shape, lambda i, be, nu: (be[i], 0, 0))
    dst_spec = lambda d: pl.BlockSpec((1, 1, rows), lambda i, be, nu: (i + d, 0, 0), memory_space=pltpu.SMEM)
    grid_spec = pltpu.PrefetchScalarGridSpec(
        num_scalar_prefetch=2,
        grid=(nb + 1,),
        in_specs=[
            dst_spec(0), dst_spec(1), dst_spec(2), dst_spec(3),
            pl.BlockSpec(memory_space=pl.ANY),
            wspec((1, D_MODEL, 2 * D_MODEL)), wspec((1, 1, 2 * D_MODEL)),
            wspec((1, D_MODEL, D_MODEL)), wspec((1, 1, D_MODEL)),
        ],
        out_specs=pl.BlockSpec(memory_space=pl.ANY),
        scratch_shapes=([pltpu.VMEM((8 * rows, LANES), F32)] * RING + [pltpu.VMEM((rows, D_MODEL), BF16)]
                        + [pltpu.VMEM((8 * rows, LANES), F32)] * RING
                        + [pltpu.VMEM((D_MODEL, 2 * D_MODEL), BF16), pltpu.VMEM((D_MODEL, D_MODEL), BF16)]
                        + [pltpu.SemaphoreType.DMA((RING,)), pltpu.SemaphoreType.DMA((RING,))]),
    )
    return pl.pallas_call(
        _experts_kernel,
        grid_spec=grid_spec,
        out_shape=jax.ShapeDtypeStruct((8 * (n_assign + rows), LANES), F32),
        compiler_params=pltpu.CompilerParams(dimension_semantics=("arbitrary",),
                                             vmem_limit_bytes=VMEM_LIMIT),
        name="experts",
    )(blk_e, nused, row_dst, row_dst, row_dst, row_dst, h1t, wgu, bgu, wd, bd)


def _combine_kernel(y0_ref, y1_ref, y2_ref, y3_ref, gate_ref, h_ref, g_ref, b_ref, out_ref):
    n = h_ref.shape[0]
    ffn = gate_ref[:, 0:1] * _from_row_tiles(y0_ref, n)
    for j, y_ref in enumerate((y1_ref, y2_ref, y3_ref), start=1):
        ffn = ffn + gate_ref[:, j:j + 1] * _from_row_tiles(y_ref, n)
    out_ref[...] = _layer_norm(DN_ALPHA * h_ref[...] + ffn, g_ref[...], b_ref[...])


def _combine_call(ys, gates, h1, g, b):
    t = h1.shape[0]
    rows = COMBINE_ROWS
    full = lambda a: pl.BlockSpec(a.shape, lambda i: (0,) * a.ndim)
    choice = lambda j: pl.BlockSpec((8 * rows, LANES), lambda i: (j * (t // rows) + i, 0))
    return pl.pallas_call(
        _combine_kernel,
        grid=(t // rows,),
        in_specs=[
            choice(0), choice(1), choice(2), choice(3),
            pl.BlockSpec((rows, LANES), lambda i: (i, 0)),
            pl.BlockSpec((rows, D_MODEL), lambda i: (i, 0)),
            full(g), full(b),
        ],
        out_specs=pl.BlockSpec((rows, D_MODEL), lambda i: (i, 0)),
        out_shape=jax.ShapeDtypeStruct((t, D_MODEL), F32),
        compiler_params=pltpu.CompilerParams(dimension_semantics=("arbitrary",),
                                             vmem_limit_bytes=VMEM_LIMIT),
        name="combine",
    )(ys, ys, ys, ys, gates, h1, g, b)


def _routing_tables(top_e, counts):
    t = top_e.shape[0]
    a = t * TOP_EXPERTS
    rows = EXPERT_ROWS
    assert t & (t - 1) == 0 and a <= (1 << ASSIGN_BITS)
    ids = jnp.arange(a, dtype=I32)
    keys = jnp.sort((top_e.T.reshape(a) << ASSIGN_BITS) | ids)
    nblk = (counts + rows - 1) // rows
    blk_end = jnp.cumsum(nblk)
    start = jnp.cumsum(counts) - counts
    nb = a // rows + N_EXPERTS
    nused = blk_end[-1].astype(I32)
    b = jnp.arange(nb + 3, dtype=I32)
    used = b < nused
    be = jnp.sum((jnp.minimum(b, nused - 1)[:, None] >= blk_end[None, :]).astype(I32), axis=1)
    onehot = (be[:, None] == jnp.arange(N_EXPERTS, dtype=I32)[None, :]).astype(I32)
    pick = lambda v: jnp.sum(onehot * v[None, :], axis=1)
    within = (b - (pick(blk_end) - pick(nblk))) * rows
    blk_start = jnp.where(used, pick(start) + within, 0).astype(I32)
    blk_n = jnp.where(used, jnp.clip(pick(counts) - within, 0, rows), 0).astype(I32)
    r = jnp.arange(rows, dtype=I32)[None, :]
    window = jnp.minimum(blk_start[:, None] + r, a - 1)
    dump = jnp.broadcast_to(a + r, (nb + 3, rows))
    row_dst = jnp.where(r < blk_n[:, None], keys[window] & ((1 << ASSIGN_BITS) - 1), dump)
    row_dst = jnp.concatenate([dump[:1], row_dst], axis=0).reshape(nb + 4, 1, rows)
    return be.astype(I32), nused.reshape(1), row_dst


def kernel(x, w_in, kv_norm_g, w_uk, w_uv, m_conv_w, m_conv_b, w_mq, w_mk, b_i, b_f, m_norm_g, w_out,
           ln1_g, ln1_b, w_router, b_router, w_gu, b_gu, w_down, b_down, ln2_g, ln2_b):
    bsz, seq, _ = x.shape
    assert seq == SEQ
    t = bsz * seq
    x2 = x.reshape(t, D_MODEL)
    w = w_in[0]
    o_qa, o_ckv, o_qi, o_ki, o_wi, o_u, o_v, o_z, o_i, o_f, o_end = 0, 512, 640, 1152, 1216, 1224, 1736, 2248, 2760, 2764, 2768
    wmain = jnp.concatenate([w[:, o_qa:o_ki], w[:, o_u:o_i]], axis=1).astype(BF16)
    wsmall = jnp.zeros((D_MODEL, LANES), F32).at[:, 0:72].set(w[:, o_ki:o_u]).at[:, 72:80].set(w[:, o_i:o_end]).astype(BF16)
    wsmallt = (jnp.zeros((LANES, D_MODEL), F32).at[0:64].set(w[:, o_ki:o_wi].T).at[64:72].set(w[:, o_i:o_end].T)
               .at[72:80].set(w[:, o_wi:o_u].T).astype(BF16))
    wckvt = w[:, o_ckv:o_qi].T.astype(BF16)
    wukt = jnp.swapaxes(w_uk[0], 1, 2).astype(BF16)

    qlat, c, ct, qidx, small, smallt, qmt, km, vt, zt = _proj_call(
        x2, wmain, wsmall, wsmallt, wckvt, wukt, kv_norm_g[0].reshape(1, KV_RANK),
        kv_norm_g[0].reshape(KV_RANK, 1), m_conv_w[0],
        m_conv_b[0].reshape(1, M_WIDTH), w_mq[0].astype(BF16), w_mk[0].astype(BF16))

    assert A_HEADS == 8
    aout = _dsa_call(qidx.reshape(bsz, seq, IDX_HEADS * IDX_DIM), smallt,
                     qlat.reshape(bsz, seq, A_HEADS * KV_RANK), small.reshape(bsz, seq, LANES),
                     c.reshape(bsz, seq, KV_RANK), ct, w_uv[0].astype(BF16))

    bias8 = jnp.concatenate([b_i[0], b_f[0]])
    bcol = jnp.zeros((1, LANES), F32).at[0, 72:80].set(bias8)
    mout = _mlstm_call(km.reshape(bsz, seq, M_WIDTH), small.reshape(bsz, seq, LANES), qmt, vt, zt, smallt,
                       bcol, bias8.reshape(8, 1), jnp.broadcast_to(m_norm_g[0][:, None], (M_WIDTH, LANES)))

    wr = jnp.zeros((D_MODEL, LANES), F32).at[:, :N_EXPERTS].set(w_router[0]).astype(BF16)
    br = jnp.full((1, LANES), NEG_BIG, F32).at[0, :N_EXPERTS].set(b_router[0])
    h1, h1t, e_slab, g_slab, cnt = _mix_call(aout.reshape(t, A_WIDTH), mout.reshape(t, M_WIDTH), x2,
                                        w_out[0].astype(BF16), ln1_g[0].reshape(1, D_MODEL),
                                        ln1_b[0].reshape(1, D_MODEL), wr, br)

    blk_e, nused, row_dst = _routing_tables(e_slab[:, :TOP_EXPERTS], cnt[0, :N_EXPERTS].astype(I32))
    ys = _experts_call(blk_e, nused, row_dst, h1t, w_gu[0],
                       b_gu[0].reshape(N_EXPERTS, 1, 2 * D_MODEL), w_down[0],
                       b_down[0].reshape(N_EXPERTS, 1, D_MODEL))
    out = _combine_call(ys, g_slab, h1, ln2_g[0].reshape(1, D_MODEL), ln2_b[0].reshape(1, D_MODEL))
    return out.reshape(bsz, seq, D_MODEL)
```

```python
import jax
import jax.numpy as jnp
from jax import lax
from jax.experimental import pallas as pl
from jax.experimental.pallas import tpu as pltpu

F32 = jnp.float32
BF16 = jnp.bfloat16
I32 = jnp.int32

D_MODEL = 1024
SEQ = 2048
CHUNK = 64
A_HEADS = 8
A_HEAD_DIM = 64
KV_RANK = 128
IDX_HEADS = 8
IDX_DIM = 64
K_SEL = 256
M_HEADS = 4
M_HEAD_DIM = 128
CONV_WIDTH = 4
N_EXPERTS = 32
TOP_EXPERTS = 4
SWIGLU_LIMIT = 7.0
SWIGLU_ALPHA = 1.702
LN_EPS = 1e-5
DN_ALPHA = 2.0 ** 0.25

A_WIDTH = A_HEADS * A_HEAD_DIM
M_WIDTH = M_HEADS * M_HEAD_DIM

LANES = 128
NEG_BIG = -1e30
INT_MIN = -(2 ** 31)

PROJ_ROWS = 512
KEY_BLOCK = 256
EXPERT_ROWS = 256
COMBINE_ROWS = 256
VMEM_LIMIT = 56 * 1024 * 1024

_QA, _CKV, _QI, _U, _V, _Z = 0, 512, 640, 1152, 1664, 2176
MAIN_WIDTH = 2688


def _dot(a, b):
    return jnp.dot(a, b, preferred_element_type=F32)


def _dot_nt(a, b):
    return lax.dot_general(a, b, (((1,), (1,)), ((), ())), preferred_element_type=F32)


def _proj_kernel(x_ref, wmain_ref, wsmall_ref, wsmallt_ref, wckvt_ref, wukt_ref, kvg_ref, kvgcol_ref,
                 convw_ref, convb_ref, wmq_ref, wmk_ref,
                 qlat_ref, c_ref, ct_ref, qidx_ref, small_ref, smallt_ref, qmt_ref, km_ref, vt_ref, zt_ref,
                 uext_ref):
    i = pl.program_id(0)
    rows = x_ref.shape[0]
    xb = x_ref[...].astype(BF16)

    qa = _dot(xb, wmain_ref[:, _QA:_CKV])
    for h in range(A_HEADS):
        qh = qa[:, h * A_HEAD_DIM:(h + 1) * A_HEAD_DIM].astype(BF16)
        ql = _dot(qh, wukt_ref[h]) * (A_HEAD_DIM ** -0.5)
        qlat_ref[:, h * KV_RANK:(h + 1) * KV_RANK] = ql.astype(BF16)

    ckv = _dot(xb, wmain_ref[:, _CKV:_QI])
    ms = jnp.mean(ckv * ckv, axis=-1, keepdims=True)
    c_ref[...] = (ckv * lax.rsqrt(ms + LN_EPS) * kvg_ref[...]).astype(BF16)
    ckv_t = _dot_nt(wckvt_ref[...], xb)
    ms_t = jnp.mean(ckv_t * ckv_t, axis=0, keepdims=True)
    ct_ref[...] = (ckv_t * lax.rsqrt(ms_t + LN_EPS) * kvgcol_ref[...]).astype(BF16)

    qidx_ref[...] = _dot(xb, wmain_ref[:, _QI:_U]).astype(BF16)
    vt_ref[...] = _dot(xb, wmain_ref[:, _V:_Z]).T.astype(BF16)
    zt_ref[...] = _dot(xb, wmain_ref[:, _Z:MAIN_WIDTH]).T
    small_ref[...] = _dot(xb, wsmall_ref[...])
    smallt_ref[...] = _dot_nt(wsmallt_ref[...], xb)

    u = _dot(xb, wmain_ref[:, _U:_V])

    @pl.when(i % (SEQ // rows) == 0)
    def _():
        uext_ref[0:8, :] = jnp.zeros((8, M_WIDTH), F32)

    uext_ref[8:8 + rows, :] = u
    acc = jnp.broadcast_to(convb_ref[...], (rows, M_WIDTH))
    for j in range(CONV_WIDTH):
        off = 8 - (CONV_WIDTH - 1) + j
        acc = acc + convw_ref[j:j + 1, :] * uext_ref[off:off + rows, :]
    uext_ref[0:8, :] = u[rows - 8:rows, :]
    uc = acc * jax.nn.sigmoid(acc)
    for h in range(M_HEADS):
        sl = slice(h * M_HEAD_DIM, (h + 1) * M_HEAD_DIM)
        uh = uc[:, sl].astype(BF16)
        qmt_ref[sl, :] = _dot(uh, wmq_ref[h]).T.astype(BF16)
        km_ref[:, sl] = (_dot(uh, wmk_ref[h]) * (M_HEAD_DIM ** -0.5)).astype(BF16)


def _proj_call(x2, wmain, wsmall, wsmallt, wckvt, wukt, kvg, kvgcol, convw, convb, wmq, wmk):
    t = x2.shape[0]
    rows = PROJ_ROWS
    full = lambda a: pl.BlockSpec(a.shape, lambda i: (0,) * a.ndim)
    tok = lambda w: pl.BlockSpec((rows, w), lambda i: (i, 0))
    tok_t = pl.BlockSpec((LANES, rows), lambda i: (0, i))
    out_shape = (
        jax.ShapeDtypeStruct((t, A_HEADS * KV_RANK), BF16),
        jax.ShapeDtypeStruct((t, KV_RANK), BF16),
        jax.ShapeDtypeStruct((KV_RANK, t), BF16),
        jax.ShapeDtypeStruct((t, IDX_HEADS * IDX_DIM), BF16),
        jax.ShapeDtypeStruct((t, LANES), F32),
        jax.ShapeDtypeStruct((LANES, t), F32),
        jax.ShapeDtypeStruct((M_WIDTH, t), BF16),
        jax.ShapeDtypeStruct((t, M_WIDTH), BF16),
        jax.ShapeDtypeStruct((M_WIDTH, t), BF16),
        jax.ShapeDtypeStruct((M_WIDTH, t), F32),
    )
    wide_t = pl.BlockSpec((M_WIDTH, rows), lambda i: (0, i))
    out_specs = (tok(A_HEADS * KV_RANK), tok(KV_RANK), tok_t, tok(IDX_HEADS * IDX_DIM), tok(LANES), tok_t,
                 wide_t, tok(M_WIDTH), wide_t, wide_t)
    return pl.pallas_call(
        _proj_kernel,
        grid=(t // rows,),
        in_specs=[tok(D_MODEL), full(wmain), full(wsmall), full(wsmallt), full(wckvt), full(wukt), full(kvg),
                  full(kvgcol), full(convw), full(convb), full(wmq), full(wmk)],
        out_specs=out_specs,
        out_shape=out_shape,
        scratch_shapes=[pltpu.VMEM((8 + rows, M_WIDTH), F32)],
        compiler_params=pltpu.CompilerParams(dimension_semantics=("arbitrary",),
                                             vmem_limit_bytes=VMEM_LIMIT),
        name="proj",
    )(x2, wmain, wsmall, wsmallt, wckvt, wukt, kvg, kvgcol, convw, convb, wmq, wmk)


QUERY_TILE = 4 * CHUNK
N_QTILES = SEQ // QUERY_TILE
N_KEY_BLOCKS = SEQ // KEY_BLOCK


def _key_positions(kb):
    return kb * KEY_BLOCK + lax.broadcasted_iota(I32, (KEY_BLOCK, QUERY_TILE), 0)


def _count_keys(sc_ref, nkb, pred):
    def body(kb, acc):
        hit = jnp.where(pred(sc_ref[kb], _key_positions(kb)), 1.0, 0.0)
        return acc + jnp.sum(hit.reshape(KEY_BLOCK // 32, 32, QUERY_TILE), axis=0)
    acc = lax.fori_loop(0, nkb, body, jnp.zeros((32, QUERY_TILE), F32))
    return jnp.sum(acc, axis=0, keepdims=True)


def _dsa_kernel(qidx_ref, wt_ref, qlat_ref, small_ref, c_ref, ct_ref, wuv_ref,
                aout_ref,
                kidx_scr, ctb_scr, sc_ref, thr_ref, jlim_ref, m_ref, l_ref, acc_ref):
    qt = pl.program_id(1)

    @pl.when(qt == 0)
    def _():
        kidx_scr[...] = small_ref[0, :, 0:IDX_DIM].astype(BF16)
        for kb in range(N_KEY_BLOCKS):
            ctb_scr[kb] = ct_ref[:, kb * KEY_BLOCK:(kb + 1) * KEY_BLOCK]

    lane = lax.broadcasted_iota(I32, (1, QUERY_TILE), 1)
    qpos = qt * QUERY_TILE + lane
    key_limit = (jnp.right_shift(qpos, CHUNK.bit_length() - 1) + 1) * CHUNK
    nkb = ((qt + 1) * QUERY_TILE + KEY_BLOCK - 1) // KEY_BLOCK
    searched = key_limit > K_SEL

    qi2 = [jnp.concatenate([qidx_ref[0, :, h * IDX_DIM:(h + 1) * IDX_DIM] for h in (2 * g, 2 * g + 1)], axis=0)
           for g in range(IDX_HEADS // 2)]
    wt = wt_ref[...]

    def score_body(kb, carry):
        kblk = kidx_scr[pl.ds(pl.multiple_of(kb * KEY_BLOCK, KEY_BLOCK), KEY_BLOCK), :]
        sc = None
        for g in range(IDX_HEADS // 2):
            rel = jnp.maximum(_dot_nt(kblk, qi2[g]), 0.0)
            for k in range(2):
                h = 2 * g + k
                term = rel[:, k * QUERY_TILE:(k + 1) * QUERY_TILE] * wt[h:h + 1, :]
                sc = term if sc is None else sc + term
        sc_ref[kb] = jnp.where(_key_positions(kb) < key_limit, sc, -jnp.inf)
        return carry

    lax.fori_loop(0, nkb, score_body, 0)

    thr_ref[...] = jnp.full(thr_ref.shape, -jnp.inf, F32)
    jlim_ref[...] = jnp.full(jlim_ref.shape, -1, I32)

    def as_score(code):
        return pltpu.bitcast(jnp.where(code < 0, code ^ jnp.int32(0x7FFFFFFF), code), F32)

    @pl.when((qt + 1) * QUERY_TILE > K_SEL)
    def _():
        k_sel = jnp.float32(K_SEL)
        cnt0 = _count_keys(sc_ref, nkb, lambda sc, pos: sc >= 0.0)
        t0 = jnp.where(cnt0 >= k_sel, jnp.int32(0), jnp.int32(INT_MIN))

        def bit_body(p, t):
            cand = t + lax.shift_left(jnp.int32(1), jnp.int32(30) - p)
            cand_f = as_score(cand)
            cnt = _count_keys(sc_ref, nkb, lambda sc, pos: sc >= cand_f)
            return jnp.where(cnt >= k_sel, cand, t)

        t = jnp.where(searched, as_score(lax.fori_loop(0, 31, bit_body, t0)), -jnp.inf)
        thr_ref[...] = jnp.broadcast_to(t, thr_ref.shape)
        jlim_ref[...] = jnp.broadcast_to(jnp.where(searched, SEQ - 1, -1), jlim_ref.shape)
        cnt_ge = jnp.where(searched, _count_keys(sc_ref, nkb, lambda sc, pos: sc >= t), 0.0)

        @pl.when(jnp.max(cnt_ge) > k_sel)
        def _():
            need = k_sel - _count_keys(sc_ref, nkb, lambda sc, pos: sc > t)

            def pos_body(p, j):
                cand = j + lax.shift_left(jnp.int32(1), jnp.int32(10) - p)
                cnt = _count_keys(sc_ref, nkb, lambda sc, pos: (sc == t) & (pos < cand))
                return jnp.where(cnt < need, cand, j)

            j = lax.fori_loop(0, 11, pos_body, jnp.zeros((1, QUERY_TILE), I32))
            jlim_ref[...] = jnp.broadcast_to(jnp.where(searched, j, -1), jlim_ref.shape)

    thr = thr_ref[0:1, :]
    jlim = jlim_ref[0:1, :]
    qlat_all = jnp.concatenate(
        [qlat_ref[0, :, h * KV_RANK:(h + 1) * KV_RANK] for h in range(A_HEADS)], axis=0)
    m_ref[...] = jnp.full(m_ref.shape, NEG_BIG, F32)
    l_ref[...] = jnp.zeros(l_ref.shape, F32)
    acc_ref[...] = jnp.zeros(acc_ref.shape, F32)

    def att_body(kb, carry):
        cblk = c_ref[0, pl.ds(pl.multiple_of(kb * KEY_BLOCK, KEY_BLOCK), KEY_BLOCK), :]
        s_all = _dot_nt(cblk, qlat_all)
        sc = sc_ref[kb]
        kpos = _key_positions(kb)
        sel = (sc > thr) | ((sc == thr) & (kpos <= jlim))
        bias = jnp.where(sel, 0.0, NEG_BIG)
        negdist = -jnp.abs(qpos - kpos).astype(F32)
        m_all = m_ref[...]
        l_all = l_ref[...]
        ps, alphas, ms, ls = [], [], [], []
        for h in range(A_HEADS):
            s = s_all[:, h * QUERY_TILE:(h + 1) * QUERY_TILE] + (2.0 ** -(h + 1)) * negdist + bias
            m_old = m_all[h:h + 1, :]
            m_new = jnp.maximum(m_old, jnp.max(s, axis=0, keepdims=True))
            alpha = jnp.exp(m_old - m_new)
            p = jnp.exp(s - m_new)
            ls.append(alpha * l_all[h:h + 1, :] + jnp.sum(p, axis=0, keepdims=True))
            ms.append(m_new)
            ps.append(p.astype(BF16))
            alphas.append(alpha)
        m_ref[...] = jnp.concatenate(ms, axis=0)
        l_ref[...] = jnp.concatenate(ls, axis=0)
        pv = _dot(ctb_scr[kb], jnp.concatenate(ps, axis=1))
        acc_ref[...] = acc_ref[...] * jnp.concatenate(alphas, axis=1) + pv
        return carry

    lax.fori_loop(0, nkb, att_body, 0)
    outs = []
    for h in range(A_HEADS):
        sl = slice(h * QUERY_TILE, (h + 1) * QUERY_TILE)
        o_t = (acc_ref[:, sl] / l_ref[h:h + 1, :]).astype(BF16)
        outs.append(lax.dot_general(o_t, wuv_ref[h], (((0,), (0,)), ((), ())), preferred_element_type=F32))
    aout_ref[0] = jnp.concatenate(outs, axis=-1).astype(BF16)


def _dsa_call(qidx, smallt, qlat, small, c, ct, wuv):
    b = qlat.shape[0]
    return pl.pallas_call(
        _dsa_kernel,
        grid=(b, N_QTILES),
        in_specs=[
            pl.BlockSpec((1, QUERY_TILE, IDX_HEADS * IDX_DIM), lambda bi, qt: (bi, qt, 0)),
            pl.BlockSpec((8, QUERY_TILE), lambda bi, qt: (9, bi * N_QTILES + qt)),
            pl.BlockSpec((1, QUERY_TILE, A_HEADS * KV_RANK), lambda bi, qt: (bi, qt, 0)),
            pl.BlockSpec((1, SEQ, LANES), lambda bi, qt: (bi, 0, 0)),
            pl.BlockSpec((1, SEQ, KV_RANK), lambda bi, qt: (bi, 0, 0)),
            pl.BlockSpec((KV_RANK, SEQ), lambda bi, qt: (0, bi)),
            pl.BlockSpec(wuv.shape, lambda bi, qt: (0, 0, 0)),
        ],
        out_specs=pl.BlockSpec((1, QUERY_TILE, A_WIDTH), lambda bi, qt: (bi, qt, 0)),
        out_shape=jax.ShapeDtypeStruct((b, SEQ, A_WIDTH), BF16),
        scratch_shapes=[pltpu.VMEM((SEQ, IDX_DIM), BF16),
                        pltpu.VMEM((N_KEY_BLOCKS, KV_RANK, KEY_BLOCK), BF16),
                        pltpu.VMEM((N_KEY_BLOCKS, KEY_BLOCK, QUERY_TILE), F32),
                        pltpu.VMEM((8, QUERY_TILE), F32), pltpu.VMEM((8, QUERY_TILE), I32),
                        pltpu.VMEM((A_HEADS, QUERY_TILE), F32), pltpu.VMEM((A_HEADS, QUERY_TILE), F32),
                        pltpu.VMEM((KV_RANK, A_HEADS * QUERY_TILE), F32)],
        compiler_params=pltpu.CompilerParams(dimension_semantics=("arbitrary", "arbitrary"),
                                             vmem_limit_bytes=VMEM_LIMIT),
        name="dsa",
    )(qidx, smallt, qlat, small, c, ct, wuv)


def _log_sigmoid(x):
    return jnp.minimum(x, 0.0) - jnp.log(1.0 + jnp.exp(-jnp.abs(x)))


MCHUNK = 128
N_MCHUNKS = SEQ // MCHUNK
MLSTM_GROUP = 2


def _mlstm_kernel(*refs):
    grp = (len(refs) - 9) // 4
    k_ref, gcol_ref = refs[0], refs[1]
    qt_refs = refs[2:2 + grp]
    vt_refs = refs[2 + grp:2 + 2 * grp]
    zt_refs = refs[2 + 2 * grp:2 + 3 * grp]
    grow_refs = refs[2 + 3 * grp:2 + 4 * grp]
    bcol_ref, brow_ref, normg_ref, out_ref, c_ref, n_ref, m_ref = refs[2 + 4 * grp:]
    ci = pl.program_id(1)

    @pl.when(ci == 0)
    def _():
        c_ref[...] = jnp.zeros(c_ref.shape, F32)
        n_ref[...] = jnp.zeros(n_ref.shape, F32)
        m_ref[...] = jnp.full(m_ref.shape, NEG_BIG, F32)

    hi = lax.Precision.HIGHEST
    r_i = lax.broadcasted_iota(I32, (MCHUNK, MCHUNK), 0)
    c_i = lax.broadcasted_iota(I32, (MCHUNK, MCHUNK), 1)
    tril = jnp.where(c_i <= r_i, 1.0, 0.0).astype(F32)
    triu = jnp.where(r_i <= c_i, 1.0, 0.0).astype(F32)
    src_before = r_i <= c_i

    for bb, h in [(bb, h) for bb in range(grp) for h in range(M_HEADS)]:
        if h == 0:
            gcol = gcol_ref[bb] + bcol_ref[...]
            grow = grow_refs[bb][...] + brow_ref[...]
            bcum_col = jnp.dot(tril, _log_sigmoid(gcol), precision=hi, preferred_element_type=F32)
            bcum_row = jnp.dot(_log_sigmoid(grow), triu, precision=hi, preferred_element_type=F32)
        st = bb * M_HEADS + h
        sl = slice(h * M_HEAD_DIM, (h + 1) * M_HEAD_DIM)
        k = k_ref[bb, :, sl]
        q_t = qt_refs[bb][sl, :]
        v_t = vt_refs[bb][sl, :]
        li_row = grow[h:h + 1, :]
        b_row = bcum_row[4 + h:5 + h, :]
        colv = gcol[:, 72 + h:73 + h] - bcum_col[:, 76 + h:77 + h]
        b_tot = b_row[:, MCHUNK - 1:MCHUNK]
        c_prev = c_ref[st]
        n_prev = n_ref[st]
        m_prev = m_ref[st][:, 0:1]

        g = b_row + m_prev
        dlog = jnp.where(src_before, b_row + colv, NEG_BIG)
        m_t = jnp.maximum(g, jnp.max(dlog, axis=0, keepdims=True))
        w_intra = jnp.exp(dlog - m_t)
        w_state = jnp.exp(g - m_t)
        s_t = _dot(k, q_t) * w_intra
        num = _dot(v_t, s_t.astype(BF16)) + w_state * _dot(c_prev.astype(BF16), q_t)
        den = jnp.sum(s_t, axis=0, keepdims=True) + w_state * _dot(n_prev.astype(BF16), q_t)[0:1, :]
        hc = num / jnp.maximum(jnp.abs(den), jnp.exp(-m_t))
        mu = jnp.mean(hc, axis=0, keepdims=True)
        var = jnp.mean(jnp.square(hc - mu), axis=0, keepdims=True)
        hc = (hc - mu) * lax.rsqrt(var + LN_EPS)
        o_t = jax.nn.sigmoid(zt_refs[bb][sl, :]) * (hc * normg_ref[sl, :])
        out_ref[bb, :, sl] = o_t.T.astype(BF16)

        a_row = b_tot - b_row + li_row
        m_new = jnp.maximum(b_tot + m_prev, jnp.max(a_row, axis=-1, keepdims=True))
        decay = jnp.exp(b_tot + m_prev - m_new)
        w_row = jnp.exp(a_row - m_new)
        c_ref[st] = decay * c_prev + _dot((v_t.astype(F32) * w_row).astype(BF16), k)
        n_ref[st] = decay * n_prev + _dot(jnp.broadcast_to(w_row, (8, MCHUNK)).astype(BF16), k)
        m_ref[st] = jnp.broadcast_to(m_new, m_ref.shape[1:])


def _mlstm_call(km, small, qmt, vt, zt, smallt, bcol, brow, normgb):
    b = km.shape[0]
    grp = MLSTM_GROUP if b % MLSTM_GROUP == 0 else 1
    tokspec = lambda w: pl.BlockSpec((grp, MCHUNK, w), lambda bi, ci: (bi, ci, 0))
    full = lambda a: pl.BlockSpec(a.shape, lambda bi, ci: (0,) * a.ndim)
    lanes = lambda rows, row_blk, bb: pl.BlockSpec(
        (rows, MCHUNK), lambda bi, ci: (row_blk, (bi * grp + bb) * N_MCHUNKS + ci))
    per_seq = lambda rows, row_blk: [lanes(rows, row_blk, bb) for bb in range(grp)]
    return pl.pallas_call(
        _mlstm_kernel,
        grid=(b // grp, N_MCHUNKS),
        in_specs=([tokspec(M_WIDTH), tokspec(LANES)] + per_seq(M_WIDTH, 0) + per_seq(M_WIDTH, 0)
                  + per_seq(M_WIDTH, 0) + per_seq(8, 8) + [full(bcol), full(brow), full(normgb)]),
        out_specs=tokspec(M_WIDTH),
        out_shape=jax.ShapeDtypeStruct((b, SEQ, M_WIDTH), BF16),
        scratch_shapes=[pltpu.VMEM((grp * M_HEADS, M_HEAD_DIM, M_HEAD_DIM), F32),
                        pltpu.VMEM((grp * M_HEADS, 8, M_HEAD_DIM), F32),
                        pltpu.VMEM((grp * M_HEADS, 1, LANES), F32)],
        compiler_params=pltpu.CompilerParams(dimension_semantics=("arbitrary", "arbitrary"),
                                             vmem_limit_bytes=VMEM_LIMIT),
        name="mlstm",
    )(km, small, *([qmt] * grp), *([vt] * grp), *([zt] * grp), *([smallt] * grp), bcol, brow, normgb)


def _layer_norm(y, g, b):
    mu = jnp.mean(y, axis=-1, keepdims=True)
    var = jnp.mean(jnp.square(y - mu), axis=-1, keepdims=True)
    return (y - mu) * lax.rsqrt(var + LN_EPS) * g + b


ROUTER_SLAB = 64


def _to_row_tiles(ref, val, row0=0):
    n = val.shape[0]
    for k in range(D_MODEL // LANES):
        ref[pl.ds(8 * row0 + k, n, stride=8), :] = val[:, k * LANES:(k + 1) * LANES]


def _from_row_tiles(ref, n):
    return jnp.concatenate([ref[pl.ds(k, n, stride=8), :] for k in range(D_MODEL // LANES)], axis=1)


def _mix_kernel(a_ref, m_ref, x_ref, wout_ref, g_ref, b_ref, wr_ref, br_ref,
                h_ref, ht_ref, e_ref, gate_ref, cnt_ref):
    @pl.when(pl.program_id(0) == 0)
    def _():
        cnt_ref[...] = jnp.zeros(cnt_ref.shape, F32)

    mix = _dot(a_ref[...], wout_ref[0:A_WIDTH, :]) + _dot(m_ref[...], wout_ref[A_WIDTH:, :])
    h1 = _layer_norm(DN_ALPHA * x_ref[...] + mix, g_ref[...], b_ref[...])
    h_ref[...] = h1
    _to_row_tiles(ht_ref, h1)
    logits_all = _dot(h1.astype(BF16), wr_ref[...]) + br_ref[...]
    slab = ROUTER_SLAB
    lane = lax.broadcasted_iota(I32, (slab, LANES), 1).astype(F32)
    cnt = jnp.zeros((1, LANES), F32)
    for r0 in range(0, logits_all.shape[0], slab):
        logits = logits_all[r0:r0 + slab]
        e_slab = jnp.zeros((slab, LANES), F32)
        v_slab = jnp.zeros((slab, LANES), F32)
        chosen = jnp.zeros((slab, LANES), F32)
        vals = []
        for j in range(TOP_EXPERTS):
            mx = jnp.max(logits, axis=-1, keepdims=True)
            idx = jnp.min(jnp.where(logits == mx, lane, float(LANES)), axis=-1, keepdims=True)
            e_slab = jnp.where(lane == float(j), idx, e_slab)
            vals.append(mx)
            chosen = jnp.where(lane == idx, 1.0, chosen)
            logits = jnp.where(lane == idx, -jnp.inf, logits)
        cnt = cnt + jnp.sum(chosen, axis=0, keepdims=True)
        ex = [jnp.exp(vj - vals[0]) for vj in vals]
        tot = ex[0] + ex[1] + ex[2] + ex[3]
        for j in range(TOP_EXPERTS):
            v_slab = jnp.where(lane == float(j), ex[j] / tot, v_slab)
        e_ref[r0:r0 + slab, :] = e_slab.astype(I32)
        gate_ref[r0:r0 + slab, :] = v_slab
    cnt_ref[...] += cnt


def _mix_call(aout, mout, x2, wout, g, b, wr, br):
    t = x2.shape[0]
    rows = PROJ_ROWS
    full = lambda a: pl.BlockSpec(a.shape, lambda i: (0,) * a.ndim)
    tok = lambda w: pl.BlockSpec((rows, w), lambda i: (i, 0))
    return pl.pallas_call(
        _mix_kernel,
        grid=(t // rows,),
        in_specs=[tok(A_WIDTH), tok(M_WIDTH), tok(D_MODEL), full(wout), full(g), full(b), full(wr), full(br)],
        out_specs=(tok(D_MODEL), pl.BlockSpec((8 * rows, LANES), lambda i: (i, 0)), tok(LANES), tok(LANES),
                   pl.BlockSpec((1, LANES), lambda i: (0, 0))),
        out_shape=(jax.ShapeDtypeStruct((t, D_MODEL), F32),
                   jax.ShapeDtypeStruct((8 * t, LANES), F32),
                   jax.ShapeDtypeStruct((t, LANES), I32),
                   jax.ShapeDtypeStruct((t, LANES), F32),
                   jax.ShapeDtypeStruct((1, LANES), F32)),
        compiler_params=pltpu.CompilerParams(dimension_semantics=("arbitrary",),
                                             vmem_limit_bytes=VMEM_LIMIT),
        name="mix",
    )(aout, mout, x2, wout, g, b, wr, br)


ASSIGN_BITS = 18
FFN_SECTIONS = 8
HID_CHUNK = D_MODEL // 4
RING = 3


def _experts_kernel(blk_e_ref, nused_ref,
                    prev_ref, cur_ref, next_ref, next2_ref, h_hbm, wgu_ref, bgu_ref, wd_ref, bd_ref,
                    ys_hbm,
                    xbuf0_ref, xbuf1_ref, xbuf2_ref, hbuf_ref, ybuf0_ref, ybuf1_ref, ybuf2_ref, wgub_ref, wdb_ref,
                    in_sem, out_sem):
    i = pl.program_id(0)
    nused = nused_ref[0]
    rows = EXPERT_ROWS
    n_tok = h_hbm.shape[0] // 8
    per_sec = rows // FFN_SECTIONS
    xbuf = (xbuf0_ref, xbuf1_ref, xbuf2_ref)
    ybuf = (ybuf0_ref, ybuf1_ref, ybuf2_ref)

    def in_wait(s):
        pltpu.make_async_copy(h_hbm.at[pl.ds(0, 8 * rows)], xbuf[s], in_sem.at[s]).wait()

    def out_wait(s):
        pltpu.make_async_copy(ybuf[s], ys_hbm.at[pl.ds(0, 8 * rows)], out_sem.at[s]).wait()

    def tile(r):
        return pl.ds(r * 8, 8) if isinstance(r, int) else pl.ds(pl.multiple_of(r * 8, 8), 8)

    def prio(r):
        return r % 2 if isinstance(r, int) else 0

    def gather_row(a, s, r):
        tok = a & (n_tok - 1)
        pltpu.make_async_copy(h_hbm.at[tile(tok)], xbuf[s].at[tile(r)], in_sem.at[s]).start(priority=prio(r))

    def scatter_row(a, s, r):
        pltpu.make_async_copy(ybuf[s].at[tile(r)], ys_hbm.at[tile(a)], out_sem.at[s]).start(priority=1 - prio(r))

    @pl.when(i == 0)
    def _():
        ybuf2_ref[...] = jnp.zeros(ybuf2_ref.shape, F32)

        def first(r, carry):
            gather_row(cur_ref[0, 0, r], 0, r)
            gather_row(next_ref[0, 0, r], 1, r)
            return carry
        lax.fori_loop(0, rows, first, 0)

    def moves(s, lo, hi):
        t = (s + 2) % RING
        for r in range(lo, hi):
            scatter_row(prev_ref[0, 0, r], t, r)
            gather_row(next2_ref[0, 0, r], t, r)

    def body(s):
        @pl.when((i == 0) | (blk_e_ref[i] != blk_e_ref[jnp.maximum(i - 1, 0)]))
        def _():
            wgub_ref[...] = wgu_ref[0].astype(BF16)
            wdb_ref[...] = wd_ref[0].astype(BF16)

        in_wait(s)
        xb = _from_row_tiles(xbuf[s], rows).astype(BF16)
        sec = 0
        for c in range(D_MODEL // HID_CHUNK):
            lo, hi = c * HID_CHUNK, (c + 1) * HID_CHUNK
            g = _dot(xb, wgub_ref[:, lo:hi]) + bgu_ref[0, :, lo:hi]
            u = _dot(xb, wgub_ref[:, D_MODEL + lo:D_MODEL + hi]) + bgu_ref[0, :, D_MODEL + lo:D_MODEL + hi]
            g = jnp.minimum(g, SWIGLU_LIMIT)
            u = jnp.clip(u, -SWIGLU_LIMIT, SWIGLU_LIMIT)
            hbuf_ref[:, lo:hi] = ((u + 1.0) * g * jax.nn.sigmoid(SWIGLU_ALPHA * g)).astype(BF16)
            moves(s, sec * per_sec, (sec + 1) * per_sec)
            sec += 1
        hid = hbuf_ref[...]
        for c in range(D_MODEL // HID_CHUNK):
            lo, hi = c * HID_CHUNK, (c + 1) * HID_CHUNK
            y = _dot(hid, wdb_ref[:, lo:hi]) + bd_ref[0, :, lo:hi]
            for k in range(HID_CHUNK // LANES):
                ybuf[s][pl.ds(lo // LANES + k, rows, stride=8), :] = y[:, k * LANES:(k + 1) * LANES]
            moves(s, sec * per_sec, (sec + 1) * per_sec)
            sec += 1

    for s in range(RING):
        @pl.when((i < nused) & (i % RING == s))
        def _(s=s):
            @pl.when(i >= 2)
            def _():
                out_wait(s)
            body(s)

        @pl.when((i == nused) & (i % RING == s))
        def _(s=s):
            in_wait(s)
            in_wait((s + 1) % RING)

            @pl.when(i >= 2)
            def _():
                out_wait(s)
            t = (s + 2) % RING

            @pl.when(i >= 1)
            def _():
                out_wait((s + 1) % RING)

            def last(r, carry):
                scatter_row(prev_ref[0, 0, r], t, r)
                return carry
            lax.fori_loop(0, rows, last, 0)
            out_wait(t)


def _experts_call(blk_e, nused, row_dst, h1t, wgu, bgu, wd, bd):
    nb = row_dst.shape[0] - 4
    rows = EXPERT_ROWS
    n_assign = h1t.shape[0] // 8 * TOP_EXPERTS
    wspec = lambda shape: pl.BlockSpec(shape, lambda i, be, nu: (be[i], 0, 0))
    dst_spec = lambda d: pl.BlockSpec((1, 1, rows), lambda i, be, nu: (i + d, 0, 0), memory_space=pltpu.SMEM)
    grid_spec = pltpu.PrefetchScalarGridSpec(
        num_scalar_prefetch=2,
        grid=(nb + 1,),
        in_specs=[
            dst_spec(0), dst_spec(1), dst_spec(2), dst_spec(3),
            pl.BlockSpec(memory_space=pl.ANY),
            wspec((1, D_MODEL, 2 * D_MODEL)), wspec((1, 1, 2 * D_MODEL)),
            wspec((1, D_MODEL, D_MODEL)), wspec((1, 1, D_MODEL)),
        ],
        out_specs=pl.BlockSpec(memory_space=pl.ANY),
        scratch_shapes=([pltpu.VMEM((8 * rows, LANES), F32)] * RING + [pltpu.VMEM((rows, D_MODEL), BF16)]
                        + [pltpu.VMEM((8 * rows, LANES), F32)] * RING
                        + [pltpu.VMEM((D_MODEL, 2 * D_MODEL), BF16), pltpu.VMEM((D_MODEL, D_MODEL), BF16)]
                        + [pltpu.SemaphoreType.DMA((RING,)), pltpu.SemaphoreType.DMA((RING,))]),
    )
    return pl.pallas_call(
        _experts_kernel,
        grid_spec=grid_spec,
        out_shape=jax.ShapeDtypeStruct((8 * (n_assign + rows), LANES), F32),
        compiler_params=pltpu.CompilerParams(dimension_semantics=("arbitrary",),
                                             vmem_limit_bytes=VMEM_LIMIT),
        name="experts",
    )(blk_e, nused, row_dst, row_dst, row_dst, row_dst, h1t, wgu, bgu, wd, bd)


def _combine_kernel(y0_ref, y1_ref, y2_ref, y3_ref, gate_ref, h_ref, g_ref, b_ref, out_ref):
    n = h_ref.shape[0]
    ffn = gate_ref[:, 0:1] * _from_row_tiles(y0_ref, n)
    for j, y_ref in enumerate((y1_ref, y2_ref, y3_ref), start=1):
        ffn = ffn + gate_ref[:, j:j + 1] * _from_row_tiles(y_ref, n)
    out_ref[...] = _layer_norm(DN_ALPHA * h_ref[...] + ffn, g_ref[...], b_ref[...])


def _combine_call(ys, gates, h1, g, b):
    t = h1.shape[0]
    rows = COMBINE_ROWS
    full = lambda a: pl.BlockSpec(a.shape, lambda i: (0,) * a.ndim)
    choice = lambda j: pl.BlockSpec((8 * rows, LANES), lambda i: (j * (t // rows) + i, 0))
    return pl.pallas_call(
        _combine_kernel,
        grid=(t // rows,),
        in_specs=[
            choice(0), choice(1), choice(2), choice(3),
            pl.BlockSpec((rows, LANES), lambda i: (i, 0)),
            pl.BlockSpec((rows, D_MODEL), lambda i: (i, 0)),
            full(g), full(b),
        ],
        out_specs=pl.BlockSpec((rows, D_MODEL), lambda i: (i, 0)),
        out_shape=jax.ShapeDtypeStruct((t, D_MODEL), F32),
        compiler_params=pltpu.CompilerParams(dimension_semantics=("arbitrary",),
                                             vmem_limit_bytes=VMEM_LIMIT),
        name="combine",
    )(ys, ys, ys, ys, gates, h1, g, b)


def _routing_tables(top_e, counts):
    t = top_e.shape[0]
    a = t * TOP_EXPERTS
    rows = EXPERT_ROWS
    assert t & (t - 1) == 0 and a <= (1 << ASSIGN_BITS)
    ids = jnp.arange(a, dtype=I32)
    keys = jnp.sort((top_e.T.reshape(a) << ASSIGN_BITS) | ids)
    nblk = (counts + rows - 1) // rows
    blk_end = jnp.cumsum(nblk)
    start = jnp.cumsum(counts) - counts
    nb = a // rows + N_EXPERTS
    nused = blk_end[-1].astype(I32)
    b = jnp.arange(nb + 3, dtype=I32)
    used = b < nused
    be = jnp.sum((jnp.minimum(b, nused - 1)[:, None] >= blk_end[None, :]).astype(I32), axis=1)
    onehot = (be[:, None] == jnp.arange(N_EXPERTS, dtype=I32)[None, :]).astype(I32)
    pick = lambda v: jnp.sum(onehot * v[None, :], axis=1)
    within = (b - (pick(blk_end) - pick(nblk))) * rows
    blk_start = jnp.where(used, pick(start) + within, 0).astype(I32)
    blk_n = jnp.where(used, jnp.clip(pick(counts) - within, 0, rows), 0).astype(I32)
    r = jnp.arange(rows, dtype=I32)[None, :]
    window = jnp.minimum(blk_start[:, None] + r, a - 1)
    dump = jnp.broadcast_to(a + r, (nb + 3, rows))
    row_dst = jnp.where(r < blk_n[:, None], keys[window] & ((1 << ASSIGN_BITS) - 1), dump)
    row_dst = jnp.concatenate([dump[:1], row_dst], axis=0).reshape(nb + 4, 1, rows)
    return be.astype(I32), nused.reshape(1), row_dst


def kernel(x, w_in, kv_norm_g, w_uk, w_uv, m_conv_w, m_conv_b, w_mq, w_mk, b_i, b_f, m_norm_g, w_out,
           ln1_g, ln1_b, w_router, b_router, w_gu, b_gu, w_down, b_down, ln2_g, ln2_b):
    bsz, seq, _ = x.shape
    assert seq == SEQ
    t = bsz * seq
    x2 = x.reshape(t, D_MODEL)
    w = w_in[0]
    o_qa, o_ckv, o_qi, o_ki, o_wi, o_u, o_v, o_z, o_i, o_f, o_end = 0, 512, 640, 1152, 1216, 1224, 1736, 2248, 2760, 2764, 2768
    wmain = jnp.concatenate([w[:, o_qa:o_ki], w[:, o_u:o_i]], axis=1).astype(BF16)
    wsmall = jnp.zeros((D_MODEL, LANES), F32).at[:, 0:72].set(w[:, o_ki:o_u]).at[:, 72:80].set(w[:, o_i:o_end]).astype(BF16)
    wsmallt = (jnp.zeros((LANES, D_MODEL), F32).at[0:64].set(w[:, o_ki:o_wi].T).at[64:72].set(w[:, o_i:o_end].T)
               .at[72:80].set(w[:, o_wi:o_u].T).astype(BF16))
    wckvt = w[:, o_ckv:o_qi].T.astype(BF16)
    wukt = jnp.swapaxes(w_uk[0], 1, 2).astype(BF16)

    qlat, c, ct, qidx, small, smallt, qmt, km, vt, zt = _proj_call(
        x2, wmain, wsmall, wsmallt, wckvt, wukt, kv_norm_g[0].reshape(1, KV_RANK),
        kv_norm_g[0].reshape(KV_RANK, 1), m_conv_w[0],
        m_conv_b[0].reshape(1, M_WIDTH), w_mq[0].astype(BF16), w_mk[0].astype(BF16))

    assert A_HEADS == 8
    aout = _dsa_call(qidx.reshape(bsz, seq, IDX_HEADS * IDX_DIM), smallt,
                     qlat.reshape(bsz, seq, A_HEADS * KV_RANK), small.reshape(bsz, seq, LANES),
                     c.reshape(bsz, seq, KV_RANK), ct, w_uv[0].astype(BF16))

    bias8 = jnp.concatenate([b_i[0], b_f[0]])
    bcol = jnp.zeros((1, LANES), F32).at[0, 72:80].set(bias8)
    mout = _mlstm_call(km.reshape(bsz, seq, M_WIDTH), small.reshape(bsz, seq, LANES), qmt, vt, zt, smallt,
                       bcol, bias8.reshape(8, 1), jnp.broadcast_to(m_norm_g[0][:, None], (M_WIDTH, LANES)))

    wr = jnp.zeros((D_MODEL, LANES), F32).at[:, :N_EXPERTS].set(w_router[0]).astype(BF16)
    br = jnp.full((1, LANES), NEG_BIG, F32).at[0, :N_EXPERTS].set(b_router[0])
    h1, h1t, e_slab, g_slab, cnt = _mix_call(aout.reshape(t, A_WIDTH), mout.reshape(t, M_WIDTH), x2,
                                        w_out[0].astype(BF16), ln1_g[0].reshape(1, D_MODEL),
                                        ln1_b[0].reshape(1, D_MODEL), wr, br)

    blk_e, nused, row_dst = _routing_tables(e_slab[:, :TOP_EXPERTS], cnt[0, :N_EXPERTS].astype(I32))
    ys = _experts_call(blk_e, nused, row_dst, h1t, w_gu[0],
                       b_gu[0].reshape(N_EXPERTS, 1, 2 * D_MODEL), w_down[0],
                       b_down[0].reshape(N_EXPERTS, 1, D_MODEL))
    out = _combine_call(ys, g_slab, h1, ln2_g[0].reshape(1, D_MODEL), ln2_b[0].reshape(1, D_MODEL))
    return out.reshape(bsz, seq, D_MODEL)
```

```python
import jax
import jax.numpy as jnp
from jax import lax
from jax.experimental import pallas as pl
from jax.experimental.pallas import tpu as pltpu

F32 = jnp.float32
BF16 = jnp.bfloat16
I32 = jnp.int32

D_MODEL = 1024
SEQ = 2048
CHUNK = 64
A_HEADS = 8
A_HEAD_DIM = 64
KV_RANK = 128
IDX_HEADS = 8
IDX_DIM = 64
K_SEL = 256
M_HEADS = 4
M_HEAD_DIM = 128
CONV_WIDTH = 4
N_EXPERTS = 32
TOP_EXPERTS = 4
SWIGLU_LIMIT = 7.0
SWIGLU_ALPHA = 1.702
LN_EPS = 1e-5
DN_ALPHA = 2.0 ** 0.25

A_WIDTH = A_HEADS * A_HEAD_DIM
M_WIDTH = M_HEADS * M_HEAD_DIM

LANES = 128
NEG_BIG = -1e30
INT_MIN = -(2 ** 31)

PROJ_ROWS = 512
KEY_BLOCK = 256
EXPERT_ROWS = 256
COMBINE_ROWS = 256
VMEM_LIMIT = 56 * 1024 * 1024

_QA, _CKV, _QI, _U, _V, _Z = 0, 512, 640, 1152, 1664, 2176
MAIN_WIDTH = 2688


def _dot(a, b):
    return jnp.dot(a, b, preferred_element_type=F32)


def _dot_nt(a, b):
    return lax.dot_general(a, b, (((1,), (1,)), ((), ())), preferred_element_type=F32)


def _proj_kernel(x_ref, wmain_ref, wsmall_ref, wsmallt_ref, wckvt_ref, wukt_ref, kvg_ref, kvgcol_ref,
                 convw_ref, convb_ref, wmq_ref, wmk_ref,
                 qlat_ref, c_ref, ct_ref, qidx_ref, small_ref, smallt_ref, qmt_ref, km_ref, vt_ref, zt_ref,
                 uext_ref):
    i = pl.program_id(0)
    rows = x_ref.shape[0]
    xb = x_ref[...].astype(BF16)

    qa = _dot(xb, wmain_ref[:, _QA:_CKV])
    for h in range(A_HEADS):
        qh = qa[:, h * A_HEAD_DIM:(h + 1) * A_HEAD_DIM].astype(BF16)
        ql = _dot(qh, wukt_ref[h]) * (A_HEAD_DIM ** -0.5)
        qlat_ref[:, h * KV_RANK:(h + 1) * KV_RANK] = ql.astype(BF16)

    ckv = _dot(xb, wmain_ref[:, _CKV:_QI])
    ms = jnp.mean(ckv * ckv, axis=-1, keepdims=True)
    c_ref[...] = (ckv * lax.rsqrt(ms + LN_EPS) * kvg_ref[...]).astype(BF16)
    ckv_t = _dot_nt(wckvt_ref[...], xb)
    ms_t = jnp.mean(ckv_t * ckv_t, axis=0, keepdims=True)
    ct_ref[...] = (ckv_t * lax.rsqrt(ms_t + LN_EPS) * kvgcol_ref[...]).astype(BF16)

    qidx_ref[...] = _dot(xb, wmain_ref[:, _QI:_U]).astype(BF16)
    vt_ref[...] = _dot(xb, wmain_ref[:, _V:_Z]).T.astype(BF16)
    zt_ref[...] = _dot(xb, wmain_ref[:, _Z:MAIN_WIDTH]).T
    small_ref[...] = _dot(xb, wsmall_ref[...])
    smallt_ref[...] = _dot_nt(wsmallt_ref[...], xb)

    u = _dot(xb, wmain_ref[:, _U:_V])

    @pl.when(i % (SEQ // rows) == 0)
    def _():
        uext_ref[0:8, :] = jnp.zeros((8, M_WIDTH), F32)

    uext_ref[8:8 + rows, :] = u
    acc = jnp.broadcast_to(convb_ref[...], (rows, M_WIDTH))
    for j in range(CONV_WIDTH):
        off = 8 - (CONV_WIDTH - 1) + j
        acc = acc + convw_ref[j:j + 1, :] * uext_ref[off:off + rows, :]
    uext_ref[0:8, :] = u[rows - 8:rows, :]
    uc = acc * jax.nn.sigmoid(acc)
    for h in range(M_HEADS):
        sl = slice(h * M_HEAD_DIM, (h + 1) * M_HEAD_DIM)
        uh = uc[:, sl].astype(BF16)
        qmt_ref[sl, :] = _dot(uh, wmq_ref[h]).T.astype(BF16)
        km_ref[:, sl] = (_dot(uh, wmk_ref[h]) * (M_HEAD_DIM ** -0.5)).astype(BF16)


def _proj_call(x2, wmain, wsmall, wsmallt, wckvt, wukt, kvg, kvgcol, convw, convb, wmq, wmk):
    t = x2.shape[0]
    rows = PROJ_ROWS
    full = lambda a: pl.BlockSpec(a.shape, lambda i: (0,) * a.ndim)
    tok = lambda w: pl.BlockSpec((rows, w), lambda i: (i, 0))
    tok_t = pl.BlockSpec((LANES, rows), lambda i: (0, i))
    out_shape = (
        jax.ShapeDtypeStruct((t, A_HEADS * KV_RANK), BF16),
        jax.ShapeDtypeStruct((t, KV_RANK), BF16),
        jax.ShapeDtypeStruct((KV_RANK, t), BF16),
        jax.ShapeDtypeStruct((t, IDX_HEADS * IDX_DIM), BF16),
        jax.ShapeDtypeStruct((t, LANES), F32),
        jax.ShapeDtypeStruct((LANES, t), F32),
        jax.ShapeDtypeStruct((M_WIDTH, t), BF16),
        jax.ShapeDtypeStruct((t, M_WIDTH), BF16),
        jax.ShapeDtypeStruct((M_WIDTH, t), BF16),
        jax.ShapeDtypeStruct((M_WIDTH, t), F32),
    )
    wide_t = pl.BlockSpec((M_WIDTH, rows), lambda i: (0, i))
    out_specs = (tok(A_HEADS * KV_RANK), tok(KV_RANK), tok_t, tok(IDX_HEADS * IDX_DIM), tok(LANES), tok_t,
                 wide_t, tok(M_WIDTH), wide_t, wide_t)
    return pl.pallas_call(
        _proj_kernel,
        grid=(t // rows,),
        in_specs=[tok(D_MODEL), full(wmain), full(wsmall), full(wsmallt), full(wckvt), full(wukt), full(kvg),
                  full(kvgcol), full(convw), full(convb), full(wmq), full(wmk)],
        out_specs=out_specs,
        out_shape=out_shape,
        scratch_shapes=[pltpu.VMEM((8 + rows, M_WIDTH), F32)],
        compiler_params=pltpu.CompilerParams(dimension_semantics=("arbitrary",),
                                             vmem_limit_bytes=VMEM_LIMIT),
        name="proj",
    )(x2, wmain, wsmall, wsmallt, wckvt, wukt, kvg, kvgcol, convw, convb, wmq, wmk)


QUERY_TILE = 4 * CHUNK
N_QTILES = SEQ // QUERY_TILE
N_KEY_BLOCKS = SEQ // KEY_BLOCK


def _key_positions(kb):
    return kb * KEY_BLOCK + lax.broadcasted_iota(I32, (KEY_BLOCK, QUERY_TILE), 0)


def _count_keys(sc_ref, nkb, pred):
    def body(kb, acc):
        hit = jnp.where(pred(sc_ref[kb], _key_positions(kb)), 1.0, 0.0)
        return acc + jnp.sum(hit.reshape(KEY_BLOCK // 32, 32, QUERY_TILE), axis=0)
    acc = lax.fori_loop(0, nkb, body, jnp.zeros((32, QUERY_TILE), F32))
    return jnp.sum(acc, axis=0, keepdims=True)


def _dsa_kernel(qidx_ref, wt_ref, qlat_ref, small_ref, c_ref, ct_ref, wuv_ref,
                aout_ref,
                kidx_scr, ctb_scr, sc_ref, thr_ref, jlim_ref, m_ref, l_ref, acc_ref):
    qt = pl.program_id(1)

    @pl.when(qt == 0)
    def _():
        kidx_scr[...] = small_ref[0, :, 0:IDX_DIM].astype(BF16)
        for kb in range(N_KEY_BLOCKS):
            ctb_scr[kb] = ct_ref[:, kb * KEY_BLOCK:(kb + 1) * KEY_BLOCK]

    lane = lax.broadcasted_iota(I32, (1, QUERY_TILE), 1)
    qpos = qt * QUERY_TILE + lane
    key_limit = (jnp.right_shift(qpos, CHUNK.bit_length() - 1) + 1) * CHUNK
    nkb = ((qt + 1) * QUERY_TILE + KEY_BLOCK - 1) // KEY_BLOCK
    searched = key_limit > K_SEL

    qi2 = [jnp.concatenate([qidx_ref[0, :, h * IDX_DIM:(h + 1) * IDX_DIM] for h in (2 * g, 2 * g + 1)], axis=0)
           for g in range(IDX_HEADS // 2)]
    wt = wt_ref[...]

    def score_body(kb, carry):
        kblk = kidx_scr[pl.ds(pl.multiple_of(kb * KEY_BLOCK, KEY_BLOCK), KEY_BLOCK), :]
        sc = None
        for g in range(IDX_HEADS // 2):
            rel = jnp.maximum(_dot_nt(kblk, qi2[g]), 0.0)
            for k in range(2):
                h = 2 * g + k
                term = rel[:, k * QUERY_TILE:(k + 1) * QUERY_TILE] * wt[h:h + 1, :]
                sc = term if sc is None else sc + term
        sc_ref[kb] = jnp.where(_key_positions(kb) < key_limit, sc, -jnp.inf)
        return carry

    lax.fori_loop(0, nkb, score_body, 0)

    thr_ref[...] = jnp.full(thr_ref.shape, -jnp.inf, F32)
    jlim_ref[...] = jnp.full(jlim_ref.shape, -1, I32)

    def as_score(code):
        return pltpu.bitcast(jnp.where(code < 0, code ^ jnp.int32(0x7FFFFFFF), code), F32)

    @pl.when((qt + 1) * QUERY_TILE > K_SEL)
    def _():
        k_sel = jnp.float32(K_SEL)
        cnt0 = _count_keys(sc_ref, nkb, lambda sc, pos: sc >= 0.0)
        t0 = jnp.where(cnt0 >= k_sel, jnp.int32(0), jnp.int32(INT_MIN))

        def bit_body(p, t):
            cand = t + lax.shift_left(jnp.int32(1), jnp.int32(30) - p)
            cand_f = as_score(cand)
            cnt = _count_keys(sc_ref, nkb, lambda sc, pos: sc >= cand_f)
            return jnp.where(cnt >= k_sel, cand, t)

        t = jnp.where(searched, as_score(lax.fori_loop(0, 31, bit_body, t0)), -jnp.inf)
        thr_ref[...] = jnp.broadcast_to(t, thr_ref.shape)
        jlim_ref[...] = jnp.broadcast_to(jnp.where(searched, SEQ - 1, -1), jlim_ref.shape)
        cnt_ge = jnp.where(searched, _count_keys(sc_ref, nkb, lambda sc, pos: sc >= t), 0.0)

        @pl.when(jnp.max(cnt_ge) > k_sel)
        def _():
            need = k_sel - _count_keys(sc_ref, nkb, lambda sc, pos: sc > t)

            def pos_body(p, j):
                cand = j + lax.shift_left(jnp.int32(1), jnp.int32(10) - p)
                cnt = _count_keys(sc_ref, nkb, lambda sc, pos: (sc == t) & (pos < cand))
                return jnp.where(cnt < need, cand, j)

            j = lax.fori_loop(0, 11, pos_body, jnp.zeros((1, QUERY_TILE), I32))
            jlim_ref[...] = jnp.broadcast_to(jnp.where(searched, j, -1), jlim_ref.shape)

    thr = thr_ref[0:1, :]
    jlim = jlim_ref[0:1, :]
    qlat_all = jnp.concatenate(
        [qlat_ref[0, :, h * KV_RANK:(h + 1) * KV_RANK] for h in range(A_HEADS)], axis=0)
    m_ref[...] = jnp.full(m_ref.shape, NEG_BIG, F32)
    l_ref[...] = jnp.zeros(l_ref.shape, F32)
    acc_ref[...] = jnp.zeros(acc_ref.shape, F32)

    def att_body(kb, carry):
        cblk = c_ref[0, pl.ds(pl.multiple_of(kb * KEY_BLOCK, KEY_BLOCK), KEY_BLOCK), :]
        s_all = _dot_nt(cblk, qlat_all)
        sc = sc_ref[kb]
        kpos = _key_positions(kb)
        sel = (sc > thr) | ((sc == thr) & (kpos <= jlim))
        bias = jnp.where(sel, 0.0, NEG_BIG)
        negdist = -jnp.abs(qpos - kpos).astype(F32)
        m_all = m_ref[...]
        l_all = l_ref[...]
        ps, alphas, ms, ls = [], [], [], []
        for h in range(A_HEADS):
            s = s_all[:, h * QUERY_TILE:(h + 1) * QUERY_TILE] + (2.0 ** -(h + 1)) * negdist + bias
            m_old = m_all[h:h + 1, :]
            m_new = jnp.maximum(m_old, jnp.max(s, axis=0, keepdims=True))
            alpha = jnp.exp(m_old - m_new)
            p = jnp.exp(s - m_new)
            ls.append(alpha * l_all[h:h + 1, :] + jnp.sum(p, axis=0, keepdims=True))
            ms.append(m_new)
            ps.append(p.astype(BF16))
            alphas.append(alpha)
        m_ref[...] = jnp.concatenate(ms, axis=0)
        l_ref[...] = jnp.concatenate(ls, axis=0)
        pv = _dot(ctb_scr[kb], jnp.concatenate(ps, axis=1))
        acc_ref[...] = acc_ref[...] * jnp.concatenate(alphas, axis=1) + pv
        return carry

    lax.fori_loop(0, nkb, att_body, 0)
    outs = []
    for h in range(A_HEADS):
        sl = slice(h * QUERY_TILE, (h + 1) * QUERY_TILE)
        o_t = (acc_ref[:, sl] / l_ref[h:h + 1, :]).astype(BF16)
        outs.append(lax.dot_general(o_t, wuv_ref[h], (((0,), (0,)), ((), ())), preferred_element_type=F32))
    aout_ref[0] = jnp.concatenate(outs, axis=-1).astype(BF16)


def _dsa_call(qidx, smallt, qlat, small, c, ct, wuv):
    b = qlat.shape[0]
    return pl.pallas_call(
        _dsa_kernel,
        grid=(b, N_QTILES),
        in_specs=[
            pl.BlockSpec((1, QUERY_TILE, IDX_HEADS * IDX_DIM), lambda bi, qt: (bi, qt, 0)),
            pl.BlockSpec((8, QUERY_TILE), lambda bi, qt: (9, bi * N_QTILES + qt)),
            pl.BlockSpec((1, QUERY_TILE, A_HEADS * KV_RANK), lambda bi, qt: (bi, qt, 0)),
            pl.BlockSpec((1, SEQ, LANES), lambda bi, qt: (bi, 0, 0)),
            pl.BlockSpec((1, SEQ, KV_RANK), lambda bi, qt: (bi, 0, 0)),
            pl.BlockSpec((KV_RANK, SEQ), lambda bi, qt: (0, bi)),
            pl.BlockSpec(wuv.shape, lambda bi, qt: (0, 0, 0)),
        ],
        out_specs=pl.BlockSpec((1, QUERY_TILE, A_WIDTH), lambda bi, qt: (bi, qt, 0)),
        out_shape=jax.ShapeDtypeStruct((b, SEQ, A_WIDTH), BF16),
        scratch_shapes=[pltpu.VMEM((SEQ, IDX_DIM), BF16),
                        pltpu.VMEM((N_KEY_BLOCKS, KV_RANK, KEY_BLOCK), BF16),
                        pltpu.VMEM((N_KEY_BLOCKS, KEY_BLOCK, QUERY_TILE), F32),
                        pltpu.VMEM((8, QUERY_TILE), F32), pltpu.VMEM((8, QUERY_TILE), I32),
                        pltpu.VMEM((A_HEADS, QUERY_TILE), F32), pltpu.VMEM((A_HEADS, QUERY_TILE), F32),
                        pltpu.VMEM((KV_RANK, A_HEADS * QUERY_TILE), F32)],
        compiler_params=pltpu.CompilerParams(dimension_semantics=("arbitrary", "arbitrary"),
                                             vmem_limit_bytes=VMEM_LIMIT),
        name="dsa",
    )(qidx, smallt, qlat, small, c, ct, wuv)


def _log_sigmoid(x):
    return jnp.minimum(x, 0.0) - jnp.log(1.0 + jnp.exp(-jnp.abs(x)))


MCHUNK = 128
N_MCHUNKS = SEQ // MCHUNK
MLSTM_GROUP = 2


def _mlstm_kernel(*refs):
    grp = (len(refs) - 9) // 4
    k_ref, gcol_ref = refs[0], refs[1]
    qt_refs = refs[2:2 + grp]
    vt_refs = refs[2 + grp:2 + 2 * grp]
    zt_refs = refs[2 + 2 * grp:2 + 3 * grp]
    grow_refs = refs[2 + 3 * grp:2 + 4 * grp]
    bcol_ref, brow_ref, normg_ref, out_ref, c_ref, n_ref, m_ref = refs[2 + 4 * grp:]
    ci = pl.program_id(1)

    @pl.when(ci == 0)
    def _():
        c_ref[...] = jnp.zeros(c_ref.shape, F32)
        n_ref[...] = jnp.zeros(n_ref.shape, F32)
        m_ref[...] = jnp.full(m_ref.shape, NEG_BIG, F32)

    hi = lax.Precision.HIGHEST
    r_i = lax.broadcasted_iota(I32, (MCHUNK, MCHUNK), 0)
    c_i = lax.broadcasted_iota(I32, (MCHUNK, MCHUNK), 1)
    tril = jnp.where(c_i <= r_i, 1.0, 0.0).astype(F32)
    triu = jnp.where(r_i <= c_i, 1.0, 0.0).astype(F32)
    src_before = r_i <= c_i

    for bb, h in [(bb, h) for bb in range(grp) for h in range(M_HEADS)]:
        if h == 0:
            gcol = gcol_ref[bb] + bcol_ref[...]
            grow = grow_refs[bb][...] + brow_ref[...]
            bcum_col = jnp.dot(tril, _log_sigmoid(gcol), precision=hi, preferred_element_type=F32)
            bcum_row = jnp.dot(_log_sigmoid(grow), triu, precision=hi, preferred_element_type=F32)
        st = bb * M_HEADS + h
        sl = slice(h * M_HEAD_DIM, (h + 1) * M_HEAD_DIM)
        k = k_ref[bb, :, sl]
        q_t = qt_refs[bb][sl, :]
        v_t = vt_refs[bb][sl, :]
        li_row = grow[h:h + 1, :]
        b_row = bcum_row[4 + h:5 + h, :]
        colv = gcol[:, 72 + h:73 + h] - bcum_col[:, 76 + h:77 + h]
        b_tot = b_row[:, MCHUNK - 1:MCHUNK]
        c_prev = c_ref[st]
        n_prev = n_ref[st]
        m_prev = m_ref[st][:, 0:1]

        g = b_row + m_prev
        dlog = jnp.where(src_before, b_row + colv, NEG_BIG)
        m_t = jnp.maximum(g, jnp.max(dlog, axis=0, keepdims=True))
        w_intra = jnp.exp(dlog - m_t)
        w_state = jnp.exp(g - m_t)
        s_t = _dot(k, q_t) * w_intra
        num = _dot(v_t, s_t.astype(BF16)) + w_state * _dot(c_prev.astype(BF16), q_t)
        den = jnp.sum(s_t, axis=0, keepdims=True) + w_state * _dot(n_prev.astype(BF16), q_t)[0:1, :]
        hc = num / jnp.maximum(jnp.abs(den), jnp.exp(-m_t))
        mu = jnp.mean(hc, axis=0, keepdims=True)
        var = jnp.mean(jnp.square(hc - mu), axis=0, keepdims=True)
        hc = (hc - mu) * lax.rsqrt(var + LN_EPS)
        o_t = jax.nn.sigmoid(zt_refs[bb][sl, :]) * (hc * normg_ref[sl, :])
        out_ref[bb, :, sl] = o_t.T.astype(BF16)

        a_row = b_tot - b_row + li_row
        m_new = jnp.maximum(b_tot + m_prev, jnp.max(a_row, axis=-1, keepdims=True))
        decay = jnp.exp(b_tot + m_prev - m_new)
        w_row = jnp.exp(a_row - m_new)
        c_ref[st] = decay * c_prev + _dot((v_t.astype(F32) * w_row).astype(BF16), k)
        n_ref[st] = decay * n_prev + _dot(jnp.broadcast_to(w_row, (8, MCHUNK)).astype(BF16), k)
        m_ref[st] = jnp.broadcast_to(m_new, m_ref.shape[1:])


def _mlstm_call(km, small, qmt, vt, zt, smallt, bcol, brow, normgb):
    b = km.shape[0]
    grp = MLSTM_GROUP if b % MLSTM_GROUP == 0 else 1
    tokspec = lambda w: pl.BlockSpec((grp, MCHUNK, w), lambda bi, ci: (bi, ci, 0))
    full = lambda a: pl.BlockSpec(a.shape, lambda bi, ci: (0,) * a.ndim)
    lanes = lambda rows, row_blk, bb: pl.BlockSpec(
        (rows, MCHUNK), lambda bi, ci: (row_blk, (bi * grp + bb) * N_MCHUNKS + ci))
    per_seq = lambda rows, row_blk: [lanes(rows, row_blk, bb) for bb in range(grp)]
    return pl.pallas_call(
        _mlstm_kernel,
        grid=(b // grp, N_MCHUNKS),
        in_specs=([tokspec(M_WIDTH), tokspec(LANES)] + per_seq(M_WIDTH, 0) + per_seq(M_WIDTH, 0)
                  + per_seq(M_WIDTH, 0) + per_seq(8, 8) + [full(bcol), full(brow), full(normgb)]),
        out_specs=tokspec(M_WIDTH),
        out_shape=jax.ShapeDtypeStruct((b, SEQ, M_WIDTH), BF16),
        scratch_shapes=[pltpu.VMEM((grp * M_HEADS, M_HEAD_DIM, M_HEAD_DIM), F32),
                        pltpu.VMEM((grp * M_HEADS, 8, M_HEAD_DIM), F32),
                        pltpu.VMEM((grp * M_HEADS, 1, LANES), F32)],
        compiler_params=pltpu.CompilerParams(dimension_semantics=("arbitrary", "arbitrary"),
                                             vmem_limit_bytes=VMEM_LIMIT),
        name="mlstm",
    )(km, small, *([qmt] * grp), *([vt] * grp), *([zt] * grp), *([smallt] * grp), bcol, brow, normgb)


def _layer_norm(y, g, b):
    mu = jnp.mean(y, axis=-1, keepdims=True)
    var = jnp.mean(jnp.square(y - mu), axis=-1, keepdims=True)
    return (y - mu) * lax.rsqrt(var + LN_EPS) * g + b


ROUTER_SLAB = 64


def _to_row_tiles(ref, val, row0=0):
    n = val.shape[0]
    for k in range(D_MODEL // LANES):
        ref[pl.ds(8 * row0 + k, n, stride=8), :] = val[:, k * LANES:(k + 1) * LANES]


def _from_row_tiles(ref, n):
    return jnp.concatenate([ref[pl.ds(k, n, stride=8), :] for k in range(D_MODEL // LANES)], axis=1)


def _mix_kernel(a_ref, m_ref, x_ref, wout_ref, g_ref, b_ref, wr_ref, br_ref,
                h_ref, ht_ref, e_ref, gate_ref, cnt_ref):
    @pl.when(pl.program_id(0) == 0)
    def _():
        cnt_ref[...] = jnp.zeros(cnt_ref.shape, F32)

    mix = _dot(a_ref[...], wout_ref[0:A_WIDTH, :]) + _dot(m_ref[...], wout_ref[A_WIDTH:, :])
    h1 = _layer_norm(DN_ALPHA * x_ref[...] + mix, g_ref[...], b_ref[...])
    h_ref[...] = h1
    _to_row_tiles(ht_ref, h1)
    logits_all = _dot(h1.astype(BF16), wr_ref[...]) + br_ref[...]
    slab = ROUTER_SLAB
    lane = lax.broadcasted_iota(I32, (slab, LANES), 1).astype(F32)
    cnt = jnp.zeros((1, LANES), F32)
    for r0 in range(0, logits_all.shape[0], slab):
        logits = logits_all[r0:r0 + slab]
        e_slab = jnp.zeros((slab, LANES), F32)
        v_slab = jnp.zeros((slab, LANES), F32)
        chosen = jnp.zeros((slab, LANES), F32)
        vals = []
        for j in range(TOP_EXPERTS):
            mx = jnp.max(logits, axis=-1, keepdims=True)
            idx = jnp.min(jnp.where(logits == mx, lane, float(LANES)), axis=-1, keepdims=True)
            e_slab = jnp.where(lane == float(j), idx, e_slab)
            vals.append(mx)
            chosen = jnp.where(lane == idx, 1.0, chosen)
            logits = jnp.where(lane == idx, -jnp.inf, logits)
        cnt = cnt + jnp.sum(chosen, axis=0, keepdims=True)
        ex = [jnp.exp(vj - vals[0]) for vj in vals]
        tot = ex[0] + ex[1] + ex[2] + ex[3]
        for j in range(TOP_EXPERTS):
            v_slab = jnp.where(lane == float(j), ex[j] / tot, v_slab)
        e_ref[r0:r0 + slab, :] = e_slab.astype(I32)
        gate_ref[r0:r0 + slab, :] = v_slab
    cnt_ref[...] += cnt


def _mix_call(aout, mout, x2, wout, g, b, wr, br):
    t = x2.shape[0]
    rows = PROJ_ROWS
    full = lambda a: pl.BlockSpec(a.shape, lambda i: (0,) * a.ndim)
    tok = lambda w: pl.BlockSpec((rows, w), lambda i: (i, 0))
    return pl.pallas_call(
        _mix_kernel,
        grid=(t // rows,),
        in_specs=[tok(A_WIDTH), tok(M_WIDTH), tok(D_MODEL), full(wout), full(g), full(b), full(wr), full(br)],
        out_specs=(tok(D_MODEL), pl.BlockSpec((8 * rows, LANES), lambda i: (i, 0)), tok(LANES), tok(LANES),
                   pl.BlockSpec((1, LANES), lambda i: (0, 0))),
        out_shape=(jax.ShapeDtypeStruct((t, D_MODEL), F32),
                   jax.ShapeDtypeStruct((8 * t, LANES), F32),
                   jax.ShapeDtypeStruct((t, LANES), I32),
                   jax.ShapeDtypeStruct((t, LANES), F32),
                   jax.ShapeDtypeStruct((1, LANES), F32)),
        compiler_params=pltpu.CompilerParams(dimension_semantics=("arbitrary",),
                                             vmem_limit_bytes=VMEM_LIMIT),
        name="mix",
    )(aout, mout, x2, wout, g, b, wr, br)


ASSIGN_BITS = 18
FFN_SECTIONS = 8
HID_CHUNK = D_MODEL // 4
RING = 3


def _experts_kernel(blk_e_ref, nused_ref,
                    prev_ref, cur_ref, next_ref, next2_ref, h_hbm, wgu_ref, bgu_ref, wd_ref, bd_ref,
                    ys_hbm,
                    xbuf0_ref, xbuf1_ref, xbuf2_ref, hbuf_ref, ybuf0_ref, ybuf1_ref, ybuf2_ref, wgub_ref, wdb_ref,
                    in_sem, out_sem):
    i = pl.program_id(0)
    nused = nused_ref[0]
    rows = EXPERT_ROWS
    n_tok = h_hbm.shape[0] // 8
    per_sec = rows // FFN_SECTIONS
    xbuf = (xbuf0_ref, xbuf1_ref, xbuf2_ref)
    ybuf = (ybuf0_ref, ybuf1_ref, ybuf2_ref)

    def in_wait(s):
        pltpu.make_async_copy(h_hbm.at[pl.ds(0, 8 * rows)], xbuf[s], in_sem.at[s]).wait()

    def out_wait(s):
        pltpu.make_async_copy(ybuf[s], ys_hbm.at[pl.ds(0, 8 * rows)], out_sem.at[s]).wait()

    def tile(r):
        return pl.ds(r * 8, 8) if isinstance(r, int) else pl.ds(pl.multiple_of(r * 8, 8), 8)

    def gather_row(a, s, r):
        tok = a & (n_tok - 1)
        pltpu.make_async_copy(h_hbm.at[tile(tok)], xbuf[s].at[tile(r)], in_sem.at[s]).start(priority=0)

    def scatter_row(a, s, r):
        pltpu.make_async_copy(ybuf[s].at[tile(r)], ys_hbm.at[tile(a)], out_sem.at[s]).start(priority=1)

    @pl.when(i == 0)
    def _():
        ybuf2_ref[...] = jnp.zeros(ybuf2_ref.shape, F32)

        def first(r, carry):
            gather_row(cur_ref[0, 0, r], 0, r)
            gather_row(next_ref[0, 0, r], 1, r)
            return carry
        lax.fori_loop(0, rows, first, 0)

    def moves(s, lo, hi):
        t = (s + 2) % RING
        for r in range(lo, hi):
            scatter_row(prev_ref[0, 0, r], t, r)
            gather_row(next2_ref[0, 0, r], t, r)

    def body(s):
        @pl.when((i == 0) | (blk_e_ref[i] != blk_e_ref[jnp.maximum(i - 1, 0)]))
        def _():
            wgub_ref[...] = wgu_ref[0].astype(BF16)
            wdb_ref[...] = wd_ref[0].astype(BF16)

        in_wait(s)
        xb = _from_row_tiles(xbuf[s], rows).astype(BF16)
        sec = 0
        for c in range(D_MODEL // HID_CHUNK):
            lo, hi = c * HID_CHUNK, (c + 1) * HID_CHUNK
            g = _dot(xb, wgub_ref[:, lo:hi]) + bgu_ref[0, :, lo:hi]
            u = _dot(xb, wgub_ref[:, D_MODEL + lo:D_MODEL + hi]) + bgu_ref[0, :, D_MODEL + lo:D_MODEL + hi]
            g = jnp.minimum(g, SWIGLU_LIMIT)
            u = jnp.clip(u, -SWIGLU_LIMIT, SWIGLU_LIMIT)
            hbuf_ref[:, lo:hi] = ((u + 1.0) * g * jax.nn.sigmoid(SWIGLU_ALPHA * g)).astype(BF16)
            moves(s, sec * per_sec, (sec + 1) * per_sec)
            sec += 1
        hid = hbuf_ref[...]
        for c in range(D_MODEL // HID_CHUNK):
            lo, hi = c * HID_CHUNK, (c + 1) * HID_CHUNK
            y = _dot(hid, wdb_ref[:, lo:hi]) + bd_ref[0, :, lo:hi]
            for k in range(HID_CHUNK // LANES):
                ybuf[s][pl.ds(lo // LANES + k, rows, stride=8), :] = y[:, k * LANES:(k + 1) * LANES]
            moves(s, sec * per_sec, (sec + 1) * per_sec)
            sec += 1

    for s in range(RING):
        @pl.when((i < nused) & (i % RING == s))
        def _(s=s):
            @pl.when(i >= 2)
            def _():
                out_wait(s)
            body(s)

        @pl.when((i == nused) & (i % RING == s))
        def _(s=s):
            in_wait(s)
            in_wait((s + 1) % RING)

            @pl.when(i >= 2)
            def _():
                out_wait(s)
            t = (s + 2) % RING

            @pl.when(i >= 1)
            def _():
                out_wait((s + 1) % RING)

            def last(r, carry):
                scatter_row(prev_ref[0, 0, r], t, r)
                return carry
            lax.fori_loop(0, rows, last, 0)
            out_wait(t)


def _experts_call(blk_e, nused, row_dst, h1t, wgu, bgu, wd, bd):
    nb = row_dst.shape[0] - 4
    rows = EXPERT_ROWS
    n_assign = h1t.shape[0] // 8 * TOP_EXPERTS
    wspec = lambda shape: pl.BlockSpec(shape, lambda i, be, nu: (be[i], 0, 0))
    dst_spec = lambda d: pl.BlockSpec((1, 1, rows), lambda i, be, nu: (i + d, 0, 0), memory_space=pltpu.SMEM)
    grid_spec = pltpu.PrefetchScalarGridSpec(
        num_scalar_prefetch=2,
        grid=(nb + 1,),
        in_specs=[
            dst_spec(0), dst_spec(1), dst_spec(2), dst_spec(3),
            pl.BlockSpec(memory_space=pl.ANY),
            wspec((1, D_MODEL, 2 * D_MODEL)), wspec((1, 1, 2 * D_MODEL)),
            wspec((1, D_MODEL, D_MODEL)), wspec((1, 1, D_MODEL)),
        ],
        out_specs=pl.BlockSpec(memory_space=pl.ANY),
        scratch_shapes=([pltpu.VMEM((8 * rows, LANES), F32)] * RING + [pltpu.VMEM((rows, D_MODEL), BF16)]
                        + [pltpu.VMEM((8 * rows, LANES), F32)] * RING
                        + [pltpu.VMEM((D_MODEL, 2 * D_MODEL), BF16), pltpu.VMEM((D_MODEL, D_MODEL), BF16)]
                        + [pltpu.SemaphoreType.DMA((RING,)), pltpu.SemaphoreType.DMA((RING,))]),
    )
    return pl.pallas_call(
        _experts_kernel,
        grid_spec=grid_spec,
        out_shape=jax.ShapeDtypeStruct((8 * (n_assign + rows), LANES), F32),
        compiler_params=pltpu.CompilerParams(dimension_semantics=("arbitrary",),
                                             vmem_limit_bytes=VMEM_LIMIT),
        name="experts",
    )(blk_e, nused, row_dst, row_dst, row_dst, row_dst, h1t, wgu, bgu, wd, bd)


def _combine_kernel(y0_ref, y1_ref, y2_ref, y3_ref, gate_ref, h_ref, g_ref, b_ref, out_ref):
    n = h_ref.shape[0]
    ffn = gate_ref[:, 0:1] * _from_row_tiles(y0_ref, n)
    for j, y_ref in enumerate((y1_ref, y2_ref, y3_ref), start=1):
        ffn = ffn + gate_ref[:, j:j + 1] * _from_row_tiles(y_ref, n)
    out_ref[...] = _layer_norm(DN_ALPHA * h_ref[...] + ffn, g_ref[...], b_ref[...])


def _combine_call(ys, gates, h1, g, b):
    t = h1.shape[0]
    rows = COMBINE_ROWS
    full = lambda a: pl.BlockSpec(a.shape, lambda i: (0,) * a.ndim)
    choice = lambda j: pl.BlockSpec((8 * rows, LANES), lambda i: (j * (t // rows) + i, 0))
    return pl.pallas_call(
        _combine_kernel,
        grid=(t // rows,),
        in_specs=[
            choice(0), choice(1), choice(2), choice(3),
            pl.BlockSpec((rows, LANES), lambda i: (i, 0)),
            pl.BlockSpec((rows, D_MODEL), lambda i: (i, 0)),
            full(g), full(b),
        ],
        out_specs=pl.BlockSpec((rows, D_MODEL), lambda i: (i, 0)),
        out_shape=jax.ShapeDtypeStruct((t, D_MODEL), F32),
        compiler_params=pltpu.CompilerParams(dimension_semantics=("arbitrary",),
                                             vmem_limit_bytes=VMEM_LIMIT),
        name="combine",
    )(ys, ys, ys, ys, gates, h1, g, b)


def _routing_tables(top_e, counts):
    t = top_e.shape[0]
    a = t * TOP_EXPERTS
    rows = EXPERT_ROWS
    assert t & (t - 1) == 0 and a <= (1 << ASSIGN_BITS)
    ids = jnp.arange(a, dtype=I32)
    keys = jnp.sort((top_e.T.reshape(a) << ASSIGN_BITS) | ids)
    nblk = (counts + rows - 1) // rows
    blk_end = jnp.cumsum(nblk)
    start = jnp.cumsum(counts) - counts
    nb = a // rows + N_EXPERTS
    nused = blk_end[-1].astype(I32)
    b = jnp.arange(nb + 3, dtype=I32)
    used = b < nused
    be = jnp.sum((jnp.minimum(b, nused - 1)[:, None] >= blk_end[None, :]).astype(I32), axis=1)
    onehot = (be[:, None] == jnp.arange(N_EXPERTS, dtype=I32)[None, :]).astype(I32)
    pick = lambda v: jnp.sum(onehot * v[None, :], axis=1)
    within = (b - (pick(blk_end) - pick(nblk))) * rows
    blk_start = jnp.where(used, pick(start) + within, 0).astype(I32)
    blk_n = jnp.where(used, jnp.clip(pick(counts) - within, 0, rows), 0).astype(I32)
    r = jnp.arange(rows, dtype=I32)[None, :]
    window = jnp.minimum(blk_start[:, None] + r, a - 1)
    dump = jnp.broadcast_to(a + r, (nb + 3, rows))
    row_dst = jnp.where(r < blk_n[:, None], keys[window] & ((1 << ASSIGN_BITS) - 1), dump)
    row_dst = jnp.concatenate([dump[:1], row_dst], axis=0).reshape(nb + 4, 1, rows)
    return be.astype(I32), nused.reshape(1), row_dst


def kernel(x, w_in, kv_norm_g, w_uk, w_uv, m_conv_w, m_conv_b, w_mq, w_mk, b_i, b_f, m_norm_g, w_out,
           ln1_g, ln1_b, w_router, b_router, w_gu, b_gu, w_down, b_down, ln2_g, ln2_b):
    bsz, seq, _ = x.shape
    assert seq == SEQ
    t = bsz * seq
    x2 = x.reshape(t, D_MODEL)
    w = w_in[0]
    o_qa, o_ckv, o_qi, o_ki, o_wi, o_u, o_v, o_z, o_i, o_f, o_end = 0, 512, 640, 1152, 1216, 1224, 1736, 2248, 2760, 2764, 2768
    wmain = jnp.concatenate([w[:, o_qa:o_ki], w[:, o_u:o_i]], axis=1).astype(BF16)
    wsmall = jnp.zeros((D_MODEL, LANES), F32).at[:, 0:72].set(w[:, o_ki:o_u]).at[:, 72:80].set(w[:, o_i:o_end]).astype(BF16)
    wsmallt = (jnp.zeros((LANES, D_MODEL), F32).at[0:64].set(w[:, o_ki:o_wi].T).at[64:72].set(w[:, o_i:o_end].T)
               .at[72:80].set(w[:, o_wi:o_u].T).astype(BF16))
    wckvt = w[:, o_ckv:o_qi].T.astype(BF16)
    wukt = jnp.swapaxes(w_uk[0], 1, 2).astype(BF16)

    qlat, c, ct, qidx, small, smallt, qmt, km, vt, zt = _proj_call(
        x2, wmain, wsmall, wsmallt, wckvt, wukt, kv_norm_g[0].reshape(1, KV_RANK),
        kv_norm_g[0].reshape(KV_RANK, 1), m_conv_w[0],
        m_conv_b[0].reshape(1, M_WIDTH), w_mq[0].astype(BF16), w_mk[0].astype(BF16))

    assert A_HEADS == 8
    aout = _dsa_call(qidx.reshape(bsz, seq, IDX_HEADS * IDX_DIM), smallt,
                     qlat.reshape(bsz, seq, A_HEADS * KV_RANK), small.reshape(bsz, seq, LANES),
                     c.reshape(bsz, seq, KV_RANK), ct, w_uv[0].astype(BF16))

    bias8 = jnp.concatenate([b_i[0], b_f[0]])
    bcol = jnp.zeros((1, LANES), F32).at[0, 72:80].set(bias8)
    mout = _mlstm_call(km.reshape(bsz, seq, M_WIDTH), small.reshape(bsz, seq, LANES), qmt, vt, zt, smallt,
                       bcol, bias8.reshape(8, 1), jnp.broadcast_to(m_norm_g[0][:, None], (M_WIDTH, LANES)))

    wr = jnp.zeros((D_MODEL, LANES), F32).at[:, :N_EXPERTS].set(w_router[0]).astype(BF16)
    br = jnp.full((1, LANES), NEG_BIG, F32).at[0, :N_EXPERTS].set(b_router[0])
    h1, h1t, e_slab, g_slab, cnt = _mix_call(aout.reshape(t, A_WIDTH), mout.reshape(t, M_WIDTH), x2,
                                        w_out[0].astype(BF16), ln1_g[0].reshape(1, D_MODEL),
                                        ln1_b[0].reshape(1, D_MODEL), wr, br)

    blk_e, nused, row_dst = _routing_tables(e_slab[:, :TOP_EXPERTS], cnt[0, :N_EXPERTS].astype(I32))
    ys = _experts_call(blk_e, nused, row_dst, h1t, w_gu[0],
                       b_gu[0].reshape(N_EXPERTS, 1, 2 * D_MODEL), w_down[0],
                       b_down[0].reshape(N_EXPERTS, 1, D_MODEL))
    out = _combine_call(ys, g_slab, h1, ln2_g[0].reshape(1, D_MODEL), ln2_b[0].reshape(1, D_MODEL))
    return out.reshape(bsz, seq, D_MODEL)
```

```python
import jax
import jax.numpy as jnp
from jax import lax
from jax.experimental import pallas as pl
from jax.experimental.pallas import tpu as pltpu

F32 = jnp.float32
BF16 = jnp.bfloat16
I32 = jnp.int32

D_MODEL = 1024
SEQ = 2048
CHUNK = 64
A_HEADS = 8
A_HEAD_DIM = 64
KV_RANK = 128
IDX_HEADS = 8
IDX_DIM = 64
K_SEL = 256
M_HEADS = 4
M_HEAD_DIM = 128
CONV_WIDTH = 4
N_EXPERTS = 32
TOP_EXPERTS = 4
SWIGLU_LIMIT = 7.0
SWIGLU_ALPHA = 1.702
LN_EPS = 1e-5
DN_ALPHA = 2.0 ** 0.25

A_WIDTH = A_HEADS * A_HEAD_DIM
M_WIDTH = M_HEADS * M_HEAD_DIM

LANES = 128
NEG_BIG = -1e30
INT_MIN = -(2 ** 31)

PROJ_ROWS = 512
KEY_BLOCK = 256
EXPERT_ROWS = 256
COMBINE_ROWS = 256
VMEM_LIMIT = 56 * 1024 * 1024

_QA, _CKV, _QI, _U, _V, _Z = 0, 512, 640, 1152, 1664, 2176
MAIN_WIDTH = 2688


def _dot(a, b):
    return jnp.dot(a, b, preferred_element_type=F32)


def _dot_nt(a, b):
    return lax.dot_general(a, b, (((1,), (1,)), ((), ())), preferred_element_type=F32)


def _proj_kernel(x_ref, wmain_ref, wsmall_ref, wsmallt_ref, wckvt_ref, wukt_ref, kvg_ref, kvgcol_ref,
                 convw_ref, convb_ref, wmq_ref, wmk_ref,
                 qlat_ref, c_ref, ct_ref, qidx_ref, small_ref, smallt_ref, qmt_ref, km_ref, vt_ref, zt_ref,
                 uext_ref):
    i = pl.program_id(0)
    rows = x_ref.shape[0]
    xb = x_ref[...].astype(BF16)

    qa = _dot(xb, wmain_ref[:, _QA:_CKV])
    for h in range(A_HEADS):
        qh = qa[:, h * A_HEAD_DIM:(h + 1) * A_HEAD_DIM].astype(BF16)
        ql = _dot(qh, wukt_ref[h]) * (A_HEAD_DIM ** -0.5)
        qlat_ref[:, h * KV_RANK:(h + 1) * KV_RANK] = ql.astype(BF16)

    ckv = _dot(xb, wmain_ref[:, _CKV:_QI])
    ms = jnp.mean(ckv * ckv, axis=-1, keepdims=True)
    c_ref[...] = (ckv * lax.rsqrt(ms + LN_EPS) * kvg_ref[...]).astype(BF16)
    ckv_t = _dot_nt(wckvt_ref[...], xb)
    ms_t = jnp.mean(ckv_t * ckv_t, axis=0, keepdims=True)
    ct_ref[...] = (ckv_t * lax.rsqrt(ms_t + LN_EPS) * kvgcol_ref[...]).astype(BF16)

    qidx_ref[...] = _dot(xb, wmain_ref[:, _QI:_U]).astype(BF16)
    vt_ref[...] = _dot(xb, wmain_ref[:, _V:_Z]).T.astype(BF16)
    zt_ref[...] = _dot(xb, wmain_ref[:, _Z:MAIN_WIDTH]).T
    small_ref[...] = _dot(xb, wsmall_ref[...])
    smallt_ref[...] = _dot_nt(wsmallt_ref[...], xb)

    u = _dot(xb, wmain_ref[:, _U:_V])

    @pl.when(i % (SEQ // rows) == 0)
    def _():
        uext_ref[0:8, :] = jnp.zeros((8, M_WIDTH), F32)

    uext_ref[8:8 + rows, :] = u
    acc = jnp.broadcast_to(convb_ref[...], (rows, M_WIDTH))
    for j in range(CONV_WIDTH):
        off = 8 - (CONV_WIDTH - 1) + j
        acc = acc + convw_ref[j:j + 1, :] * uext_ref[off:off + rows, :]
    uext_ref[0:8, :] = u[rows - 8:rows, :]
    uc = acc * jax.nn.sigmoid(acc)
    for h in range(M_HEADS):
        sl = slice(h * M_HEAD_DIM, (h + 1) * M_HEAD_DIM)
        uh = uc[:, sl].astype(BF16)
        qmt_ref[sl, :] = _dot(uh, wmq_ref[h]).T.astype(BF16)
        km_ref[:, sl] = (_dot(uh, wmk_ref[h]) * (M_HEAD_DIM ** -0.5)).astype(BF16)


def _proj_call(x2, wmain, wsmall, wsmallt, wckvt, wukt, kvg, kvgcol, convw, convb, wmq, wmk):
    t = x2.shape[0]
    rows = PROJ_ROWS
    full = lambda a: pl.BlockSpec(a.shape, lambda i: (0,) * a.ndim)
    tok = lambda w: pl.BlockSpec((rows, w), lambda i: (i, 0))
    tok_t = pl.BlockSpec((LANES, rows), lambda i: (0, i))
    out_shape = (
        jax.ShapeDtypeStruct((t, A_HEADS * KV_RANK), BF16),
        jax.ShapeDtypeStruct((t, KV_RANK), BF16),
        jax.ShapeDtypeStruct((KV_RANK, t), BF16),
        jax.ShapeDtypeStruct((t, IDX_HEADS * IDX_DIM), BF16),
        jax.ShapeDtypeStruct((t, LANES), F32),
        jax.ShapeDtypeStruct((LANES, t), F32),
        jax.ShapeDtypeStruct((M_WIDTH, t), BF16),
        jax.ShapeDtypeStruct((t, M_WIDTH), BF16),
        jax.ShapeDtypeStruct((M_WIDTH, t), BF16),
        jax.ShapeDtypeStruct((M_WIDTH, t), F32),
    )
    wide_t = pl.BlockSpec((M_WIDTH, rows), lambda i: (0, i))
    out_specs = (tok(A_HEADS * KV_RANK), tok(KV_RANK), tok_t, tok(IDX_HEADS * IDX_DIM), tok(LANES), tok_t,
                 wide_t, tok(M_WIDTH), wide_t, wide_t)
    return pl.pallas_call(
        _proj_kernel,
        grid=(t // rows,),
        in_specs=[tok(D_MODEL), full(wmain), full(wsmall), full(wsmallt), full(wckvt), full(wukt), full(kvg),
                  full(kvgcol), full(convw), full(convb), full(wmq), full(wmk)],
        out_specs=out_specs,
        out_shape=out_shape,
        scratch_shapes=[pltpu.VMEM((8 + rows, M_WIDTH), F32)],
        compiler_params=pltpu.CompilerParams(dimension_semantics=("arbitrary",),
                                             vmem_limit_bytes=VMEM_LIMIT),
        name="proj",
    )(x2, wmain, wsmall, wsmallt, wckvt, wukt, kvg, kvgcol, convw, convb, wmq, wmk)


QUERY_TILE = 4 * CHUNK
N_QTILES = SEQ // QUERY_TILE
N_KEY_BLOCKS = SEQ // KEY_BLOCK


def _key_positions(kb):
    return kb * KEY_BLOCK + lax.broadcasted_iota(I32, (KEY_BLOCK, QUERY_TILE), 0)


def _count_keys(sc_ref, nkb, pred):
    def body(kb, acc):
        hit = jnp.where(pred(sc_ref[kb], _key_positions(kb)), 1.0, 0.0)
        return acc + jnp.sum(hit.reshape(KEY_BLOCK // 32, 32, QUERY_TILE), axis=0)
    acc = lax.fori_loop(0, nkb, body, jnp.zeros((32, QUERY_TILE), F32))
    return jnp.sum(acc, axis=0, keepdims=True)


def _dsa_kernel(qidx_ref, wt_ref, qlat_ref, small_ref, c_ref, ct_ref, wuv_ref,
                aout_ref,
                kidx_scr, ctb_scr, sc_ref, thr_ref, jlim_ref, m_ref, l_ref, acc_ref):
    qt = pl.program_id(1)

    @pl.when(qt == 0)
    def _():
        kidx_scr[...] = small_ref[0, :, 0:IDX_DIM].astype(BF16)
        for kb in range(N_KEY_BLOCKS):
            ctb_scr[kb] = ct_ref[:, kb * KEY_BLOCK:(kb + 1) * KEY_BLOCK]

    lane = lax.broadcasted_iota(I32, (1, QUERY_TILE), 1)
    qpos = qt * QUERY_TILE + lane
    key_limit = (jnp.right_shift(qpos, CHUNK.bit_length() - 1) + 1) * CHUNK
    nkb = ((qt + 1) * QUERY_TILE + KEY_BLOCK - 1) // KEY_BLOCK
    searched = key_limit > K_SEL

    qi2 = [jnp.concatenate([qidx_ref[0, :, h * IDX_DIM:(h + 1) * IDX_DIM] for h in (2 * g, 2 * g + 1)], axis=0)
           for g in range(IDX_HEADS // 2)]
    wt = wt_ref[...]

    def score_body(kb, carry):
        kblk = kidx_scr[pl.ds(pl.multiple_of(kb * KEY_BLOCK, KEY_BLOCK), KEY_BLOCK), :]
        sc = None
        for g in range(IDX_HEADS // 2):
            rel = jnp.maximum(_dot_nt(kblk, qi2[g]), 0.0)
            for k in range(2):
                h = 2 * g + k
                term = rel[:, k * QUERY_TILE:(k + 1) * QUERY_TILE] * wt[h:h + 1, :]
                sc = term if sc is None else sc + term
        sc_ref[kb] = jnp.where(_key_positions(kb) < key_limit, sc, -jnp.inf)
        return carry

    lax.fori_loop(0, nkb, score_body, 0)

    thr_ref[...] = jnp.full(thr_ref.shape, -jnp.inf, F32)
    jlim_ref[...] = jnp.full(jlim_ref.shape, -1, I32)

    def as_score(code):
        return pltpu.bitcast(jnp.where(code < 0, code ^ jnp.int32(0x7FFFFFFF), code), F32)

    @pl.when((qt + 1) * QUERY_TILE > K_SEL)
    def _():
        k_sel = jnp.float32(K_SEL)
        cnt0 = _count_keys(sc_ref, nkb, lambda sc, pos: sc >= 0.0)
        t0 = jnp.where(cnt0 >= k_sel, jnp.int32(0), jnp.int32(INT_MIN))

        def bit_body(p, t):
            cand = t + lax.shift_left(jnp.int32(1), jnp.int32(30) - p)
            cand_f = as_score(cand)
            cnt = _count_keys(sc_ref, nkb, lambda sc, pos: sc >= cand_f)
            return jnp.where(cnt >= k_sel, cand, t)

        t = jnp.where(searched, as_score(lax.fori_loop(0, 31, bit_body, t0)), -jnp.inf)
        thr_ref[...] = jnp.broadcast_to(t, thr_ref.shape)
        jlim_ref[...] = jnp.broadcast_to(jnp.where(searched, SEQ - 1, -1), jlim_ref.shape)
        cnt_ge = jnp.where(searched, _count_keys(sc_ref, nkb, lambda sc, pos: sc >= t), 0.0)

        @pl.when(jnp.max(cnt_ge) > k_sel)
        def _():
            need = k_sel - _count_keys(sc_ref, nkb, lambda sc, pos: sc > t)

            def pos_body(p, j):
                cand = j + lax.shift_left(jnp.int32(1), jnp.int32(10) - p)
                cnt = _count_keys(sc_ref, nkb, lambda sc, pos: (sc == t) & (pos < cand))
                return jnp.where(cnt < need, cand, j)

            j = lax.fori_loop(0, 11, pos_body, jnp.zeros((1, QUERY_TILE), I32))
            jlim_ref[...] = jnp.broadcast_to(jnp.where(searched, j, -1), jlim_ref.shape)

    thr = thr_ref[0:1, :]
    jlim = jlim_ref[0:1, :]
    qlat_all = jnp.concatenate(
        [qlat_ref[0, :, h * KV_RANK:(h + 1) * KV_RANK] for h in range(A_HEADS)], axis=0)
    m_ref[...] = jnp.full(m_ref.shape, NEG_BIG, F32)
    l_ref[...] = jnp.zeros(l_ref.shape, F32)
    acc_ref[...] = jnp.zeros(acc_ref.shape, F32)

    def att_body(kb, carry):
        cblk = c_ref[0, pl.ds(pl.multiple_of(kb * KEY_BLOCK, KEY_BLOCK), KEY_BLOCK), :]
        s_all = _dot_nt(cblk, qlat_all)
        sc = sc_ref[kb]
        kpos = _key_positions(kb)
        sel = (sc > thr) | ((sc == thr) & (kpos <= jlim))
        bias = jnp.where(sel, 0.0, NEG_BIG)
        negdist = -jnp.abs(qpos - kpos).astype(F32)
        m_all = m_ref[...]
        l_all = l_ref[...]
        ps, alphas, ms, ls = [], [], [], []
        for h in range(A_HEADS):
            s = s_all[:, h * QUERY_TILE:(h + 1) * QUERY_TILE] + (2.0 ** -(h + 1)) * negdist + bias
            m_old = m_all[h:h + 1, :]
            m_new = jnp.maximum(m_old, jnp.max(s, axis=0, keepdims=True))
            alpha = jnp.exp(m_old - m_new)
            p = jnp.exp(s - m_new)
            ls.append(alpha * l_all[h:h + 1, :] + jnp.sum(p, axis=0, keepdims=True))
            ms.append(m_new)
            ps.append(p.astype(BF16))
            alphas.append(alpha)
        m_ref[...] = jnp.concatenate(ms, axis=0)
        l_ref[...] = jnp.concatenate(ls, axis=0)
        pv = _dot(ctb_scr[kb], jnp.concatenate(ps, axis=1))
        acc_ref[...] = acc_ref[...] * jnp.concatenate(alphas, axis=1) + pv
        return carry

    lax.fori_loop(0, nkb, att_body, 0)
    outs = []
    for h in range(A_HEADS):
        sl = slice(h * QUERY_TILE, (h + 1) * QUERY_TILE)
        o_t = (acc_ref[:, sl] / l_ref[h:h + 1, :]).astype(BF16)
        outs.append(lax.dot_general(o_t, wuv_ref[h], (((0,), (0,)), ((), ())), preferred_element_type=F32))
    aout_ref[0] = jnp.concatenate(outs, axis=-1).astype(BF16)


def _dsa_call(qidx, smallt, qlat, small, c, ct, wuv):
    b = qlat.shape[0]
    return pl.pallas_call(
        _dsa_kernel,
        grid=(b, N_QTILES),
        in_specs=[
            pl.BlockSpec((1, QUERY_TILE, IDX_HEADS * IDX_DIM), lambda bi, qt: (bi, qt, 0)),
            pl.BlockSpec((8, QUERY_TILE), lambda bi, qt: (9, bi * N_QTILES + qt)),
            pl.BlockSpec((1, QUERY_TILE, A_HEADS * KV_RANK), lambda bi, qt: (bi, qt, 0)),
            pl.BlockSpec((1, SEQ, LANES), lambda bi, qt: (bi, 0, 0)),
            pl.BlockSpec((1, SEQ, KV_RANK), lambda bi, qt: (bi, 0, 0)),
            pl.BlockSpec((KV_RANK, SEQ), lambda bi, qt: (0, bi)),
            pl.BlockSpec(wuv.shape, lambda bi, qt: (0, 0, 0)),
        ],
        out_specs=pl.BlockSpec((1, QUERY_TILE, A_WIDTH), lambda bi, qt: (bi, qt, 0)),
        out_shape=jax.ShapeDtypeStruct((b, SEQ, A_WIDTH), BF16),
        scratch_shapes=[pltpu.VMEM((SEQ, IDX_DIM), BF16),
                        pltpu.VMEM((N_KEY_BLOCKS, KV_RANK, KEY_BLOCK), BF16),
                        pltpu.VMEM((N_KEY_BLOCKS, KEY_BLOCK, QUERY_TILE), F32),
                        pltpu.VMEM((8, QUERY_TILE), F32), pltpu.VMEM((8, QUERY_TILE), I32),
                        pltpu.VMEM((A_HEADS, QUERY_TILE), F32), pltpu.VMEM((A_HEADS, QUERY_TILE), F32),
                        pltpu.VMEM((KV_RANK, A_HEADS * QUERY_TILE), F32)],
        compiler_params=pltpu.CompilerParams(dimension_semantics=("arbitrary", "arbitrary"),
                                             vmem_limit_bytes=VMEM_LIMIT),
        name="dsa",
    )(qidx, smallt, qlat, small, c, ct, wuv)


def _log_sigmoid(x):
    return jnp.minimum(x, 0.0) - jnp.log(1.0 + jnp.exp(-jnp.abs(x)))


MCHUNK = 128
N_MCHUNKS = SEQ // MCHUNK
MLSTM_GROUP = 4


def _mlstm_kernel(*refs):
    grp = (len(refs) - 9) // 4
    k_ref, gcol_ref = refs[0], refs[1]
    qt_refs = refs[2:2 + grp]
    vt_refs = refs[2 + grp:2 + 2 * grp]
    zt_refs = refs[2 + 2 * grp:2 + 3 * grp]
    grow_refs = refs[2 + 3 * grp:2 + 4 * grp]
    bcol_ref, brow_ref, normg_ref, out_ref, c_ref, n_ref, m_ref = refs[2 + 4 * grp:]
    ci = pl.program_id(1)

    @pl.when(ci == 0)
    def _():
        c_ref[...] = jnp.zeros(c_ref.shape, F32)
        n_ref[...] = jnp.zeros(n_ref.shape, F32)
        m_ref[...] = jnp.full(m_ref.shape, NEG_BIG, F32)

    hi = lax.Precision.HIGHEST
    r_i = lax.broadcasted_iota(I32, (MCHUNK, MCHUNK), 0)
    c_i = lax.broadcasted_iota(I32, (MCHUNK, MCHUNK), 1)
    tril = jnp.where(c_i <= r_i, 1.0, 0.0).astype(F32)
    triu = jnp.where(r_i <= c_i, 1.0, 0.0).astype(F32)
    src_before = r_i <= c_i

    for bb, h in [(bb, h) for bb in range(grp) for h in range(M_HEADS)]:
        if h == 0:
            gcol = gcol_ref[bb] + bcol_ref[...]
            grow = grow_refs[bb][...] + brow_ref[...]
            bcum_col = jnp.dot(tril, _log_sigmoid(gcol), precision=hi, preferred_element_type=F32)
            bcum_row = jnp.dot(_log_sigmoid(grow), triu, precision=hi, preferred_element_type=F32)
        st = bb * M_HEADS + h
        sl = slice(h * M_HEAD_DIM, (h + 1) * M_HEAD_DIM)
        k = k_ref[bb, :, sl]
        q_t = qt_refs[bb][sl, :]
        v_t = vt_refs[bb][sl, :]
        li_row = grow[h:h + 1, :]
        b_row = bcum_row[4 + h:5 + h, :]
        colv = gcol[:, 72 + h:73 + h] - bcum_col[:, 76 + h:77 + h]
        b_tot = b_row[:, MCHUNK - 1:MCHUNK]
        c_prev = c_ref[st]
        n_prev = n_ref[st]
        m_prev = m_ref[st][:, 0:1]

        g = b_row + m_prev
        dlog = jnp.where(src_before, b_row + colv, NEG_BIG)
        m_t = jnp.maximum(g, jnp.max(dlog, axis=0, keepdims=True))
        w_intra = jnp.exp(dlog - m_t)
        w_state = jnp.exp(g - m_t)
        s_t = _dot(k, q_t) * w_intra
        num = _dot(v_t, s_t.astype(BF16)) + w_state * _dot(c_prev.astype(BF16), q_t)
        den = jnp.sum(s_t, axis=0, keepdims=True) + w_state * _dot(n_prev.astype(BF16), q_t)[0:1, :]
        hc = num / jnp.maximum(jnp.abs(den), jnp.exp(-m_t))
        mu = jnp.mean(hc, axis=0, keepdims=True)
        var = jnp.mean(jnp.square(hc - mu), axis=0, keepdims=True)
        hc = (hc - mu) * lax.rsqrt(var + LN_EPS)
        o_t = jax.nn.sigmoid(zt_refs[bb][sl, :]) * (hc * normg_ref[sl, :])
        out_ref[bb, :, sl] = o_t.T.astype(BF16)

        a_row = b_tot - b_row + li_row
        m_new = jnp.maximum(b_tot + m_prev, jnp.max(a_row, axis=-1, keepdims=True))
        decay = jnp.exp(b_tot + m_prev - m_new)
        w_row = jnp.exp(a_row - m_new)
        c_ref[st] = decay * c_prev + _dot((v_t.astype(F32) * w_row).astype(BF16), k)
        n_ref[st] = decay * n_prev + _dot(jnp.broadcast_to(w_row, (8, MCHUNK)).astype(BF16), k)
        m_ref[st] = jnp.broadcast_to(m_new, m_ref.shape[1:])


def _mlstm_call(km, small, qmt, vt, zt, smallt, bcol, brow, normgb):
    b = km.shape[0]
    grp = MLSTM_GROUP if b % MLSTM_GROUP == 0 else 1
    tokspec = lambda w: pl.BlockSpec((grp, MCHUNK, w), lambda bi, ci: (bi, ci, 0))
    full = lambda a: pl.BlockSpec(a.shape, lambda bi, ci: (0,) * a.ndim)
    lanes = lambda rows, row_blk, bb: pl.BlockSpec(
        (rows, MCHUNK), lambda bi, ci: (row_blk, (bi * grp + bb) * N_MCHUNKS + ci))
    per_seq = lambda rows, row_blk: [lanes(rows, row_blk, bb) for bb in range(grp)]
    return pl.pallas_call(
        _mlstm_kernel,
        grid=(b // grp, N_MCHUNKS),
        in_specs=([tokspec(M_WIDTH), tokspec(LANES)] + per_seq(M_WIDTH, 0) + per_seq(M_WIDTH, 0)
                  + per_seq(M_WIDTH, 0) + per_seq(8, 8) + [full(bcol), full(brow), full(normgb)]),
        out_specs=tokspec(M_WIDTH),
        out_shape=jax.ShapeDtypeStruct((b, SEQ, M_WIDTH), BF16),
        scratch_shapes=[pltpu.VMEM((grp * M_HEADS, M_HEAD_DIM, M_HEAD_DIM), F32),
                        pltpu.VMEM((grp * M_HEADS, 8, M_HEAD_DIM), F32),
                        pltpu.VMEM((grp * M_HEADS, 1, LANES), F32)],
        compiler_params=pltpu.CompilerParams(dimension_semantics=("arbitrary", "arbitrary"),
                                             vmem_limit_bytes=VMEM_LIMIT),
        name="mlstm",
    )(km, small, *([qmt] * grp), *([vt] * grp), *([zt] * grp), *([smallt] * grp), bcol, brow, normgb)


def _layer_norm(y, g, b):
    mu = jnp.mean(y, axis=-1, keepdims=True)
    var = jnp.mean(jnp.square(y - mu), axis=-1, keepdims=True)
    return (y - mu) * lax.rsqrt(var + LN_EPS) * g + b


ROUTER_SLAB = 64


def _to_row_tiles(ref, val, row0=0):
    n = val.shape[0]
    for k in range(D_MODEL // LANES):
        ref[pl.ds(8 * row0 + k, n, stride=8), :] = val[:, k * LANES:(k + 1) * LANES]


def _from_row_tiles(ref, n):
    return jnp.concatenate([ref[pl.ds(k, n, stride=8), :] for k in range(D_MODEL // LANES)], axis=1)


def _mix_kernel(a_ref, m_ref, x_ref, wout_ref, g_ref, b_ref, wr_ref, br_ref,
                h_ref, ht_ref, e_ref, gate_ref, cnt_ref):
    @pl.when(pl.program_id(0) == 0)
    def _():
        cnt_ref[...] = jnp.zeros(cnt_ref.shape, F32)

    mix = _dot(a_ref[...], wout_ref[0:A_WIDTH, :]) + _dot(m_ref[...], wout_ref[A_WIDTH:, :])
    h1 = _layer_norm(DN_ALPHA * x_ref[...] + mix, g_ref[...], b_ref[...])
    h_ref[...] = h1
    _to_row_tiles(ht_ref, h1)
    logits_all = _dot(h1.astype(BF16), wr_ref[...]) + br_ref[...]
    slab = ROUTER_SLAB
    lane = lax.broadcasted_iota(I32, (slab, LANES), 1).astype(F32)
    cnt = jnp.zeros((1, LANES), F32)
    for r0 in range(0, logits_all.shape[0], slab):
        logits = logits_all[r0:r0 + slab]
        e_slab = jnp.zeros((slab, LANES), F32)
        v_slab = jnp.zeros((slab, LANES), F32)
        chosen = jnp.zeros((slab, LANES), F32)
        vals = []
        for j in range(TOP_EXPERTS):
            mx = jnp.max(logits, axis=-1, keepdims=True)
            idx = jnp.min(jnp.where(logits == mx, lane, float(LANES)), axis=-1, keepdims=True)
            e_slab = jnp.where(lane == float(j), idx, e_slab)
            vals.append(mx)
            chosen = jnp.where(lane == idx, 1.0, chosen)
            logits = jnp.where(lane == idx, -jnp.inf, logits)
        cnt = cnt + jnp.sum(chosen, axis=0, keepdims=True)
        ex = [jnp.exp(vj - vals[0]) for vj in vals]
        tot = ex[0] + ex[1] + ex[2] + ex[3]
        for j in range(TOP_EXPERTS):
            v_slab = jnp.where(lane == float(j), ex[j] / tot, v_slab)
        e_ref[r0:r0 + slab, :] = e_slab.astype(I32)
        gate_ref[r0:r0 + slab, :] = v_slab
    cnt_ref[...] += cnt


def _mix_call(aout, mout, x2, wout, g, b, wr, br):
    t = x2.shape[0]
    rows = PROJ_ROWS
    full = lambda a: pl.BlockSpec(a.shape, lambda i: (0,) * a.ndim)
    tok = lambda w: pl.BlockSpec((rows, w), lambda i: (i, 0))
    return pl.pallas_call(
        _mix_kernel,
        grid=(t // rows,),
        in_specs=[tok(A_WIDTH), tok(M_WIDTH), tok(D_MODEL), full(wout), full(g), full(b), full(wr), full(br)],
        out_specs=(tok(D_MODEL), pl.BlockSpec((8 * rows, LANES), lambda i: (i, 0)), tok(LANES), tok(LANES),
                   pl.BlockSpec((1, LANES), lambda i: (0, 0))),
        out_shape=(jax.ShapeDtypeStruct((t, D_MODEL), F32),
                   jax.ShapeDtypeStruct((8 * t, LANES), F32),
                   jax.ShapeDtypeStruct((t, LANES), I32),
                   jax.ShapeDtypeStruct((t, LANES), F32),
                   jax.ShapeDtypeStruct((1, LANES), F32)),
        compiler_params=pltpu.CompilerParams(dimension_semantics=("arbitrary",),
                                             vmem_limit_bytes=VMEM_LIMIT),
        name="mix",
    )(aout, mout, x2, wout, g, b, wr, br)


ASSIGN_BITS = 18
FFN_SECTIONS = 8
HID_CHUNK = D_MODEL // 4
RING = 3


def _experts_kernel(blk_e_ref, nused_ref,
                    prev_ref, cur_ref, next_ref, next2_ref, h_hbm, wgu_ref, bgu_ref, wd_ref, bd_ref,
                    ys_hbm,
                    xbuf0_ref, xbuf1_ref, xbuf2_ref, hbuf_ref, ybuf0_ref, ybuf1_ref, ybuf2_ref, wgub_ref, wdb_ref,
                    in_sem, out_sem):
    i = pl.program_id(0)
    nused = nused_ref[0]
    rows = EXPERT_ROWS
    n_tok = h_hbm.shape[0] // 8
    per_sec = rows // FFN_SECTIONS
    xbuf = (xbuf0_ref, xbuf1_ref, xbuf2_ref)
    ybuf = (ybuf0_ref, ybuf1_ref, ybuf2_ref)

    def in_wait(s):
        pltpu.make_async_copy(h_hbm.at[pl.ds(0, 8 * rows)], xbuf[s], in_sem.at[s]).wait()

    def out_wait(s):
        pltpu.make_async_copy(ybuf[s], ys_hbm.at[pl.ds(0, 8 * rows)], out_sem.at[s]).wait()

    def tile(r):
        return pl.ds(r * 8, 8) if isinstance(r, int) else pl.ds(pl.multiple_of(r * 8, 8), 8)

    def gather_row(a, s, r):
        tok = a & (n_tok - 1)
        pltpu.make_async_copy(h_hbm.at[tile(tok)], xbuf[s].at[tile(r)], in_sem.at[s]).start(priority=0)

    def scatter_row(a, s, r):
        pltpu.make_async_copy(ybuf[s].at[tile(r)], ys_hbm.at[tile(a)], out_sem.at[s]).start(priority=1)

    @pl.when(i == 0)
    def _():
        ybuf2_ref[...] = jnp.zeros(ybuf2_ref.shape, F32)

        def first(r, carry):
            gather_row(cur_ref[0, 0, r], 0, r)
            gather_row(next_ref[0, 0, r], 1, r)
            return carry
        lax.fori_loop(0, rows, first, 0)

    def moves(s, lo, hi):
        t = (s + 2) % RING
        for r in range(lo, hi):
            scatter_row(prev_ref[0, 0, r], t, r)
            gather_row(next2_ref[0, 0, r], t, r)

    def body(s):
        @pl.when((i == 0) | (blk_e_ref[i] != blk_e_ref[jnp.maximum(i - 1, 0)]))
        def _():
            wgub_ref[...] = wgu_ref[0].astype(BF16)
            wdb_ref[...] = wd_ref[0].astype(BF16)

        in_wait(s)
        xb = _from_row_tiles(xbuf[s], rows).astype(BF16)
        sec = 0
        for c in range(D_MODEL // HID_CHUNK):
            lo, hi = c * HID_CHUNK, (c + 1) * HID_CHUNK
            g = _dot(xb, wgub_ref[:, lo:hi]) + bgu_ref[0, :, lo:hi]
            u = _dot(xb, wgub_ref[:, D_MODEL + lo:D_MODEL + hi]) + bgu_ref[0, :, D_MODEL + lo:D_MODEL + hi]
            g = jnp.minimum(g, SWIGLU_LIMIT)
            u = jnp.clip(u, -SWIGLU_LIMIT, SWIGLU_LIMIT)
            hbuf_ref[:, lo:hi] = ((u + 1.0) * g * jax.nn.sigmoid(SWIGLU_ALPHA * g)).astype(BF16)
            moves(s, sec * per_sec, (sec + 1) * per_sec)
            sec += 1
        hid = hbuf_ref[...]
        for c in range(D_MODEL // HID_CHUNK):
            lo, hi = c * HID_CHUNK, (c + 1) * HID_CHUNK
            y = _dot(hid, wdb_ref[:, lo:hi]) + bd_ref[0, :, lo:hi]
            for k in range(HID_CHUNK // LANES):
                ybuf[s][pl.ds(lo // LANES + k, rows, stride=8), :] = y[:, k * LANES:(k + 1) * LANES]
            moves(s, sec * per_sec, (sec + 1) * per_sec)
            sec += 1

    for s in range(RING):
        @pl.when((i < nused) & (i % RING == s))
        def _(s=s):
            @pl.when(i >= 2)
            def _():
                out_wait(s)
            body(s)

        @pl.when((i == nused) & (i % RING == s))
        def _(s=s):
            in_wait(s)
            in_wait((s + 1) % RING)

            @pl.when(i >= 2)
            def _():
                out_wait(s)
            t = (s + 2) % RING

            @pl.when(i >= 1)
            def _():
                out_wait((s + 1) % RING)

            def last(r, carry):
                scatter_row(prev_ref[0, 0, r], t, r)
                return carry
            lax.fori_loop(0, rows, last, 0)
            out_wait(t)


def _experts_call(blk_e, nused, row_dst, h1t, wgu, bgu, wd, bd):
    nb = row_dst.shape[0] - 4
    rows = EXPERT_ROWS
    n_assign = h1t.shape[0] // 8 * TOP_EXPERTS
    wspec = lambda shape: pl.BlockSpec(shape, lambda i, be, nu: (be[i], 0, 0))
    dst_spec = lambda d: pl.BlockSpec((1, 1, rows), lambda i, be, nu: (i + d, 0, 0), memory_space=pltpu.SMEM)
    grid_spec = pltpu.PrefetchScalarGridSpec(
        num_scalar_prefetch=2,
        grid=(nb + 1,),
        in_specs=[
            dst_spec(0), dst_spec(1), dst_spec(2), dst_spec(3),
            pl.BlockSpec(memory_space=pl.ANY),
            wspec((1, D_MODEL, 2 * D_MODEL)), wspec((1, 1, 2 * D_MODEL)),
            wspec((1, D_MODEL, D_MODEL)), wspec((1, 1, D_MODEL)),
        ],
        out_specs=pl.BlockSpec(memory_space=pl.ANY),
        scratch_shapes=([pltpu.VMEM((8 * rows, LANES), F32)] * RING + [pltpu.VMEM((rows, D_MODEL), BF16)]
                        + [pltpu.VMEM((8 * rows, LANES), F32)] * RING
                        + [pltpu.VMEM((D_MODEL, 2 * D_MODEL), BF16), pltpu.VMEM((D_MODEL, D_MODEL), BF16)]
                        + [pltpu.SemaphoreType.DMA((RING,)), pltpu.SemaphoreType.DMA((RING,))]),
    )
    return pl.pallas_call(
        _experts_kernel,
        grid_spec=grid_spec,
        out_shape=jax.ShapeDtypeStruct((8 * (n_assign + rows), LANES), F32),
        compiler_params=pltpu.CompilerParams(dimension_semantics=("arbitrary",),
                                             vmem_limit_bytes=VMEM_LIMIT),
        name="experts",
    )(blk_e, nused, row_dst, row_dst, row_dst, row_dst, h1t, wgu, bgu, wd, bd)


def _combine_kernel(y0_ref, y1_ref, y2_ref, y3_ref, gate_ref, h_ref, g_ref, b_ref, out_ref):
    n = h_ref.shape[0]
    ffn = gate_ref[:, 0:1] * _from_row_tiles(y0_ref, n)
    for j, y_ref in enumerate((y1_ref, y2_ref, y3_ref), start=1):
        ffn = ffn + gate_ref[:, j:j + 1] * _from_row_tiles(y_ref, n)
    out_ref[...] = _layer_norm(DN_ALPHA * h_ref[...] + ffn, g_ref[...], b_ref[...])


def _combine_call(ys, gates, h1, g, b):
    t = h1.shape[0]
    rows = COMBINE_ROWS
    full = lambda a: pl.BlockSpec(a.shape, lambda i: (0,) * a.ndim)
    choice = lambda j: pl.BlockSpec((8 * rows, LANES), lambda i: (j * (t // rows) + i, 0))
    return pl.pallas_call(
        _combine_kernel,
        grid=(t // rows,),
        in_specs=[
            choice(0), choice(1), choice(2), choice(3),
            pl.BlockSpec((rows, LANES), lambda i: (i, 0)),
            pl.BlockSpec((rows, D_MODEL), lambda i: (i, 0)),
            full(g), full(b),
        ],
        out_specs=pl.BlockSpec((rows, D_MODEL), lambda i: (i, 0)),
        out_shape=jax.ShapeDtypeStruct((t, D_MODEL), F32),
        compiler_params=pltpu.CompilerParams(dimension_semantics=("arbitrary",),
                                             vmem_limit_bytes=VMEM_LIMIT),
        name="combine",
    )(ys, ys, ys, ys, gates, h1, g, b)


def _routing_tables(top_e, counts):
    t = top_e.shape[0]
    a = t * TOP_EXPERTS
    rows = EXPERT_ROWS
    assert t & (t - 1) == 0 and a <= (1 << ASSIGN_BITS)
    ids = jnp.arange(a, dtype=I32)
    keys = jnp.sort((top_e.T.reshape(a) << ASSIGN_BITS) | ids)
    nblk = (counts + rows - 1) // rows
    blk_end = jnp.cumsum(nblk)
    start = jnp.cumsum(counts) - counts
    nb = a // rows + N_EXPERTS
    nused = blk_end[-1].astype(I32)
    b = jnp.arange(nb + 3, dtype=I32)
    used = b < nused
    be = jnp.sum((jnp.minimum(b, nused - 1)[:, None] >= blk_end[None, :]).astype(I32), axis=1)
    onehot = (be[:, None] == jnp.arange(N_EXPERTS, dtype=I32)[None, :]).astype(I32)
    pick = lambda v: jnp.sum(onehot * v[None, :], axis=1)
    within = (b - (pick(blk_end) - pick(nblk))) * rows
    blk_start = jnp.where(used, pick(start) + within, 0).astype(I32)
    blk_n = jnp.where(used, jnp.clip(pick(counts) - within, 0, rows), 0).astype(I32)
    r = jnp.arange(rows, dtype=I32)[None, :]
    window = jnp.minimum(blk_start[:, None] + r, a - 1)
    dump = jnp.broadcast_to(a + r, (nb + 3, rows))
    row_dst = jnp.where(r < blk_n[:, None], keys[window] & ((1 << ASSIGN_BITS) - 1), dump)
    row_dst = jnp.concatenate([dump[:1], row_dst], axis=0).reshape(nb + 4, 1, rows)
    return be.astype(I32), nused.reshape(1), row_dst


def kernel(x, w_in, kv_norm_g, w_uk, w_uv, m_conv_w, m_conv_b, w_mq, w_mk, b_i, b_f, m_norm_g, w_out,
           ln1_g, ln1_b, w_router, b_router, w_gu, b_gu, w_down, b_down, ln2_g, ln2_b):
    bsz, seq, _ = x.shape
    assert seq == SEQ
    t = bsz * seq
    x2 = x.reshape(t, D_MODEL)
    w = w_in[0]
    o_qa, o_ckv, o_qi, o_ki, o_wi, o_u, o_v, o_z, o_i, o_f, o_end = 0, 512, 640, 1152, 1216, 1224, 1736, 2248, 2760, 2764, 2768
    wmain = jnp.concatenate([w[:, o_qa:o_ki], w[:, o_u:o_i]], axis=1).astype(BF16)
    wsmall = jnp.zeros((D_MODEL, LANES), F32).at[:, 0:72].set(w[:, o_ki:o_u]).at[:, 72:80].set(w[:, o_i:o_end]).astype(BF16)
    wsmallt = (jnp.zeros((LANES, D_MODEL), F32).at[0:64].set(w[:, o_ki:o_wi].T).at[64:72].set(w[:, o_i:o_end].T)
               .at[72:80].set(w[:, o_wi:o_u].T).astype(BF16))
    wckvt = w[:, o_ckv:o_qi].T.astype(BF16)
    wukt = jnp.swapaxes(w_uk[0], 1, 2).astype(BF16)

    qlat, c, ct, qidx, small, smallt, qmt, km, vt, zt = _proj_call(
        x2, wmain, wsmall, wsmallt, wckvt, wukt, kv_norm_g[0].reshape(1, KV_RANK),
        kv_norm_g[0].reshape(KV_RANK, 1), m_conv_w[0],
        m_conv_b[0].reshape(1, M_WIDTH), w_mq[0].astype(BF16), w_mk[0].astype(BF16))

    assert A_HEADS == 8
    aout = _dsa_call(qidx.reshape(bsz, seq, IDX_HEADS * IDX_DIM), smallt,
                     qlat.reshape(bsz, seq, A_HEADS * KV_RANK), small.reshape(bsz, seq, LANES),
                     c.reshape(bsz, seq, KV_RANK), ct, w_uv[0].astype(BF16))

    bias8 = jnp.concatenate([b_i[0], b_f[0]])
    bcol = jnp.zeros((1, LANES), F32).at[0, 72:80].set(bias8)
    mout = _mlstm_call(km.reshape(bsz, seq, M_WIDTH), small.reshape(bsz, seq, LANES), qmt, vt, zt, smallt,
                       bcol, bias8.reshape(8, 1), jnp.broadcast_to(m_norm_g[0][:, None], (M_WIDTH, LANES)))

    wr = jnp.zeros((D_MODEL, LANES), F32).at[:, :N_EXPERTS].set(w_router[0]).astype(BF16)
    br = jnp.full((1, LANES), NEG_BIG, F32).at[0, :N_EXPERTS].set(b_router[0])
    h1, h1t, e_slab, g_slab, cnt = _mix_call(aout.reshape(t, A_WIDTH), mout.reshape(t, M_WIDTH), x2,
                                        w_out[0].astype(BF16), ln1_g[0].reshape(1, D_MODEL),
                                        ln1_b[0].reshape(1, D_MODEL), wr, br)

    blk_e, nused, row_dst = _routing_tables(e_slab[:, :TOP_EXPERTS], cnt[0, :N_EXPERTS].astype(I32))
    ys = _experts_call(blk_e, nused, row_dst, h1t, w_gu[0],
                       b_gu[0].reshape(N_EXPERTS, 1, 2 * D_MODEL), w_down[0],
                       b_down[0].reshape(N_EXPERTS, 1, D_MODEL))
    out = _combine_call(ys, g_slab, h1, ln2_g[0].reshape(1, D_MODEL), ln2_b[0].reshape(1, D_MODEL))
    return out.reshape(bsz, seq, D_MODEL)
```
